```python
import math
import jax, jax.numpy as jnp
from jax import lax
import numpy as np

D_MODEL = 4096
BATCH = 1
SEQ = 8192
DEPTH = 1

N_HEADS = 16
HEAD_DIM = 128
V_DIM = 2 * HEAD_DIM
ATTN_WIDTH = N_HEADS * V_DIM
CONV_CH = D_MODEL
CONV_WIDTH = 31
CONV_PAD = CONV_WIDTH // 2
D_FF = 4 * D_MODEL
PLE_DIM = 256
REL_BUCKETS = 32
REL_MAX_DIST = 128
Q_BLOCK = 128
EPS = 1e-6

Q_COLS = 2 * N_HEADS * HEAD_DIM
K_COLS = 2 * N_HEADS * HEAD_DIM
V_COLS = N_HEADS * V_DIM
CONV_COLS = 2 * CONV_CH
GATE_COLS = 2 * D_MODEL
IN_COLS = Q_COLS + K_COLS + V_COLS + CONV_COLS + GATE_COLS
SPLITS = (Q_COLS, Q_COLS + K_COLS, Q_COLS + K_COLS + V_COLS,
          Q_COLS + K_COLS + V_COLS + CONV_COLS)

kernel_name = "hybrid_diffattn_conformer_gated_encoder_layer"


def rms_norm(x, g):
    xf = x.astype(jnp.float32)
    y = xf * lax.rsqrt(jnp.mean(xf * xf, axis=-1, keepdims=True) + EPS)
    return (y * g.astype(jnp.float32)).astype(x.dtype)


def layer_norm(x, g, b):
    xf = x.astype(jnp.float32)
    mu = jnp.mean(xf, axis=-1, keepdims=True)
    var = jnp.mean(jnp.square(xf - mu), axis=-1, keepdims=True)
    y = (xf - mu) * lax.rsqrt(var + EPS)
    return (y * g.astype(jnp.float32) + b.astype(jnp.float32)).astype(x.dtype)


def rel_bucket(rel):
    nb = REL_BUCKETS // 2
    max_exact = nb // 2
    ret = (rel > 0).astype(jnp.int32) * nb
    n = jnp.abs(rel)
    nf = jnp.maximum(n, 1).astype(jnp.float32)
    large = max_exact + (jnp.log(nf / max_exact) / math.log(REL_MAX_DIST / max_exact)
                         * (nb - max_exact)).astype(jnp.int32)
    large = jnp.minimum(large, nb - 1)
    return ret + jnp.where(n < max_exact, n, large)


def diff_attention(q1, q2, k1, k2, v, positions, rel_table, lam):
    B, H, S, _ = q1.shape
    nblk = S // Q_BLOCK
    scale = HEAD_DIM ** -0.5

    def to_blocks(t):
        return t.reshape(B, H, nblk, Q_BLOCK, t.shape[-1]).transpose(2, 0, 1, 3, 4)

    qb1, qb2 = to_blocks(q1), to_blocks(q2)
    pb = positions.reshape(B, nblk, Q_BLOCK).transpose(1, 0, 2)

    def one_block(args):
        bq1, bq2, bp = args
        rel = positions[:, None, :] - bp[:, :, None]
        bias = jnp.moveaxis(rel_table[rel_bucket(rel)], -1, 1).astype(jnp.float32)
        s1 = jnp.einsum('bhqd,bhkd->bhqk', bq1, k1).astype(jnp.float32) * scale + bias
        s2 = jnp.einsum('bhqd,bhkd->bhqk', bq2, k2).astype(jnp.float32) * scale + bias
        a = jax.nn.softmax(s1, axis=-1) - lam * jax.nn.softmax(s2, axis=-1)
        return jnp.einsum('bhqk,bhkd->bhqd', a.astype(v.dtype), v)

    out = lax.map(one_block, (qb1, qb2, pb))
    return out.transpose(1, 2, 0, 3, 4).reshape(B, H, S, V_DIM)


def setup_inputs(seed: int = 0) -> dict:
    key = jax.random.key(seed)
    ks = jax.random.split(key, 32)
    f32 = jnp.float32

    def nrm(k, shape, scale):
        return jax.random.normal(k, shape, f32) * scale

    def gain(k, shape):
        return 1.0 + 0.01 * jax.random.normal(k, shape, f32)

    L = DEPTH
    return {
        "x": jax.random.normal(ks[0], (BATCH, SEQ, D_MODEL), f32),
        "p": jax.random.normal(ks[1], (DEPTH, BATCH, SEQ, PLE_DIM), f32),
        "positions": jnp.broadcast_to(jnp.arange(SEQ, dtype=jnp.int32)[None, :], (BATCH, SEQ)),
        "rel_table": nrm(ks[2], (REL_BUCKETS, N_HEADS), 0.5),
        "mix_pre_g": gain(ks[3], (L, D_MODEL)),
        "w_in": nrm(ks[4], (L, D_MODEL, IN_COLS), D_MODEL ** -0.5),
        "lambda_q1": nrm(ks[5], (L, HEAD_DIM), 0.1),
        "lambda_k1": nrm(ks[6], (L, HEAD_DIM), 0.1),
        "lambda_q2": nrm(ks[7], (L, HEAD_DIM), 0.1),
        "lambda_k2": nrm(ks[8], (L, HEAD_DIM), 0.1),
        "subln_g": gain(ks[9], (L, V_DIM)),
        "w_attn_o": nrm(ks[10], (L, ATTN_WIDTH, D_MODEL), ATTN_WIDTH ** -0.5),
        "w_dw": nrm(ks[11], (L, CONV_WIDTH, 1, CONV_CH), CONV_WIDTH ** -0.5),
        "b_dw": nrm(ks[12], (L, CONV_CH), 0.01),
        "conv_ln_g": gain(ks[13], (L, CONV_CH)),
        "conv_ln_b": nrm(ks[14], (L, CONV_CH), 0.01),
        "w_conv_o": nrm(ks[15], (L, CONV_CH, D_MODEL), CONV_CH ** -0.5),
        "w_out": nrm(ks[16], (L, D_MODEL, D_MODEL), D_MODEL ** -0.5),
        "mix_post_g": gain(ks[17], (L, D_MODEL)),
        "ffn_pre_g": gain(ks[18], (L, D_MODEL)),
        "w_up": nrm(ks[19], (L, D_MODEL, D_FF), D_MODEL ** -0.5),
        "w_down": nrm(ks[20], (L, D_FF, D_MODEL), D_FF ** -0.5),
        "ffn_post_g": gain(ks[21], (L, D_MODEL)),
        "w_ple_gate": nrm(ks[22], (L, D_MODEL, D_MODEL), D_MODEL ** -0.5),
        "w_ple_proj": nrm(ks[23], (L, PLE_DIM, D_MODEL), PLE_DIM ** -0.5),
        "ple_post_g": gain(ks[24], (L, D_MODEL)),
    }


def reference(x, p, positions, rel_table, mix_pre_g, w_in, lambda_q1, lambda_k1,
              lambda_q2, lambda_k2, subln_g, w_attn_o, w_dw, b_dw, conv_ln_g,
              conv_ln_b, w_conv_o, w_out, mix_post_g, ffn_pre_g, w_up, w_down,
              ffn_post_g, w_ple_gate, w_ple_proj, ple_post_g):
    B, S, _ = x.shape
    for i in range(DEPTH):
        lambda_init = 0.8 - 0.6 * math.exp(-0.3 * i)

        h = rms_norm(x, mix_pre_g[i])
        cols = h @ w_in[i]
        q, k, v, conv_in, gate_in = jnp.split(cols, SPLITS, axis=-1)

        q = q.reshape(B, S, N_HEADS, 2, HEAD_DIM)
        k = k.reshape(B, S, N_HEADS, 2, HEAD_DIM)
        q1 = q[:, :, :, 0].transpose(0, 2, 1, 3)
        q2 = q[:, :, :, 1].transpose(0, 2, 1, 3)
        k1 = k[:, :, :, 0].transpose(0, 2, 1, 3)
        k2 = k[:, :, :, 1].transpose(0, 2, 1, 3)
        vh = v.reshape(B, S, N_HEADS, V_DIM).transpose(0, 2, 1, 3)
        lam = (jnp.exp(jnp.sum(lambda_q1[i].astype(jnp.float32) * lambda_k1[i].astype(jnp.float32)))
               - jnp.exp(jnp.sum(lambda_q2[i].astype(jnp.float32) * lambda_k2[i].astype(jnp.float32)))
               + lambda_init)
        o = diff_attention(q1, q2, k1, k2, vh, positions, rel_table, lam)
        o = rms_norm(o, subln_g[i]) * (1.0 - lambda_init)
        o = o.transpose(0, 2, 1, 3).reshape(B, S, ATTN_WIDTH)
        y_attn = o @ w_attn_o[i]

        c_val, c_gate = jnp.split(conv_in, 2, axis=-1)
        c = c_val * jax.nn.sigmoid(c_gate)
        c = lax.conv_general_dilated(
            c, w_dw[i], window_strides=(1,), padding=[(CONV_PAD, CONV_PAD)],
            dimension_numbers=('NWC', 'WIO', 'NWC'), feature_group_count=CONV_CH) + b_dw[i]
        c = jax.nn.silu(layer_norm(c, conv_ln_g[i], conv_ln_b[i]))
        y_conv = c @ w_conv_o[i]

        g_attn, g_conv = jnp.split(jax.nn.sigmoid(gate_in.astype(jnp.float32)).astype(x.dtype), 2, axis=-1)
        mix = (g_attn * y_attn + g_conv * y_conv) @ w_out[i]
        x = x + rms_norm(mix, mix_post_g[i])

        h = rms_norm(x, ffn_pre_g[i])
        u = jnp.square(jax.nn.relu(h @ w_up[i]))
        x = x + rms_norm(u @ w_down[i], ffn_post_g[i])

        ple_gate = jax.nn.sigmoid((x @ w_ple_gate[i]).astype(jnp.float32)).astype(x.dtype)
        e = p[i] @ w_ple_proj[i]
        x = x + rms_norm(ple_gate * e, ple_post_g[i])
    return x
```

```python
import functools
import math

import jax
import jax.numpy as jnp
from jax import lax
from jax.experimental import pallas as pl
from jax.experimental.pallas import tpu as pltpu

N_HEADS = 16
HEAD_DIM = 128
V_DIM = 2 * HEAD_DIM
CONV_WIDTH = 31
CONV_PAD = CONV_WIDTH // 2
REL_BUCKETS = 32
REL_MAX_DIST = 128
EPS = 1e-6
LAMBDA_INIT = 0.8 - 0.6 * math.exp(-0.3 * 0)

LANES = 128
SUBLANES = 8
VMEM_LIMIT_BYTES = 56 * 1024 * 1024

F32 = jnp.float32
BF16 = jnp.bfloat16


def _cparams(sem):
    return pltpu.CompilerParams(dimension_semantics=sem,
                                vmem_limit_bytes=VMEM_LIMIT_BYTES)


def _sigmoid(x):
    return 1.0 / (1.0 + jnp.exp(-x))


def _rms(x, g):
    return x * lax.rsqrt(jnp.mean(x * x, axis=-1, keepdims=True) + EPS) * g


def _prenorm_kernel(x_ref, g_ref, h_ref):
    h_ref[...] = _rms(x_ref[...], g_ref[...]).astype(h_ref.dtype)


def _prenorm(x, g, tr=256):
    S, D = x.shape
    return pl.pallas_call(
        _prenorm_kernel,
        grid=(S // tr,),
        in_specs=[pl.BlockSpec((tr, D), lambda i: (i, 0)),
                  pl.BlockSpec((1, D), lambda i: (0, 0))],
        out_specs=pl.BlockSpec((tr, D), lambda i: (i, 0)),
        out_shape=jax.ShapeDtypeStruct((S, D), BF16),
        compiler_params=_cparams(("parallel",)),
        name="prenorm",
    )(x, g.reshape(1, D))


def _postnorm_res_kernel(x_ref, y_ref, gpost_ref, *rest, with_prenorm):
    x_new = x_ref[...] + _rms(y_ref[...], gpost_ref[...])
    if with_prenorm:
        gpre_ref, xo_ref, h_ref = rest
        xo_ref[...] = x_new
        h_ref[...] = _rms(x_new, gpre_ref[...]).astype(h_ref.dtype)
    else:
        xo_ref, h_ref = rest
        xo_ref[...] = x_new
        h_ref[...] = x_new.astype(h_ref.dtype)


def _postnorm_res(x, y, g_post, g_pre=None, tr=256):
    S, D = x.shape
    row = pl.BlockSpec((tr, D), lambda i: (i, 0))
    vec = pl.BlockSpec((1, D), lambda i: (0, 0))
    args = [x, y, g_post.reshape(1, D)]
    in_specs = [row, row, vec]
    if g_pre is not None:
        args.append(g_pre.reshape(1, D))
        in_specs.append(vec)
    return pl.pallas_call(
        functools.partial(_postnorm_res_kernel, with_prenorm=g_pre is not None),
        grid=(S // tr,),
        in_specs=in_specs,
        out_specs=[row, row],
        out_shape=[jax.ShapeDtypeStruct((S, D), F32),
                   jax.ShapeDtypeStruct((S, D), BF16)],
        compiler_params=_cparams(("parallel",)),
        name="postnorm_res",
    )(*args)


def _final_kernel(x_ref, y_ref, g_ref, o_ref):
    o_ref[...] = x_ref[...] + _rms(y_ref[...], g_ref[...])


def _final_res(x, y, g, tr=256):
    S, D = x.shape
    row = pl.BlockSpec((tr, D), lambda i: (i, 0))
    return pl.pallas_call(
        _final_kernel,
        grid=(S // tr,),
        in_specs=[row, row, pl.BlockSpec((1, D), lambda i: (0, 0))],
        out_specs=row,
        out_shape=jax.ShapeDtypeStruct((S, D), F32),
        compiler_params=_cparams(("parallel",)),
        name="final_res",
    )(x, y, g.reshape(1, D))


def _fused_mm_kernel(*refs, n_pairs, epilogue):
    out_ref = refs[-1]
    accs = [jnp.dot(refs[2 * i][...], refs[2 * i + 1][...],
                    preferred_element_type=F32) for i in range(n_pairs)]
    extras = [r[...] for r in refs[2 * n_pairs:-1]]
    out_ref[...] = epilogue(*accs, *extras).astype(out_ref.dtype)


def _fused_mm(pairs, extras, epilogue, n_out, out_dtype, tm, tn, name):
    M = pairs[0][0].shape[0]
    in_specs, args = [], []
    for a, w, off in pairs:
        K = a.shape[1]
        ob = off // tn
        in_specs.append(pl.BlockSpec((tm, K), lambda m, n: (m, 0)))
        in_specs.append(pl.BlockSpec((K, tn), lambda m, n, ob=ob: (0, n + ob)))
        args += [a, w]
    for e, off in extras:
        ob = off // tn
        in_specs.append(pl.BlockSpec((tm, tn), lambda m, n, ob=ob: (m, n + ob)))
        args.append(e)
    return pl.pallas_call(
        functools.partial(_fused_mm_kernel, n_pairs=len(pairs), epilogue=epilogue),
        grid=(M // tm, n_out // tn),
        in_specs=in_specs,
        out_specs=pl.BlockSpec((tm, tn), lambda m, n: (m, n)),
        out_shape=jax.ShapeDtypeStruct((M, n_out), out_dtype),
        compiler_params=_cparams(("parallel", "arbitrary")),
        name=name,
    )(*args)


def _mm_ktiled_kernel(a_ref, w_ref, o_ref, acc_ref):
    k = pl.program_id(2)

    @pl.when(k == 0)
    def _():
        acc_ref[...] = jnp.zeros_like(acc_ref)

    acc_ref[...] += jnp.dot(a_ref[...], w_ref[...], preferred_element_type=F32)

    @pl.when(k == pl.num_programs(2) - 1)
    def _():
        o_ref[...] = acc_ref[...]


def _mm_ktiled(a, w, tm, tn, tk, name):
    M, K = a.shape
    N = w.shape[1]
    return pl.pallas_call(
        _mm_ktiled_kernel,
        grid=(M // tm, N // tn, K // tk),
        in_specs=[pl.BlockSpec((tm, tk), lambda m, n, k: (m, k)),
                  pl.BlockSpec((tk, tn), lambda m, n, k: (k, n))],
        out_specs=pl.BlockSpec((tm, tn), lambda m, n, k: (m, n)),
        out_shape=jax.ShapeDtypeStruct((M, N), F32),
        scratch_shapes=[pltpu.VMEM((tm, tn), F32)],
        compiler_params=_cparams(("parallel", "parallel", "arbitrary")),
        name=name,
    )(a, w)


def _rel_bucket(rel):
    nb = REL_BUCKETS // 2
    max_exact = nb // 2
    ret = jnp.where(rel > 0, nb, 0)
    n = jnp.abs(rel)
    nf = jnp.maximum(n, 1).astype(F32)
    large = max_exact + (jnp.log(nf / max_exact) / math.log(REL_MAX_DIST / max_exact)
                         * (nb - max_exact)).astype(jnp.int32)
    large = jnp.minimum(large, nb - 1)
    return ret + jnp.where(n < max_exact, n, large)


def _softmax_update(s, bias, v, m_ref, l_ref, acc_ref):
    t = s * (HEAD_DIM ** -0.5) + bias
    m_prev = m_ref[...]
    m_new = jnp.maximum(m_prev, jnp.max(t, axis=-1, keepdims=True))
    alpha = jnp.exp(m_prev - m_new)
    p = jnp.exp(t - m_new)
    l_ref[...] = alpha * l_ref[...] + jnp.sum(p, axis=-1, keepdims=True)
    acc_ref[...] = alpha * acc_ref[...] + jnp.dot(
        p.astype(v.dtype), v, preferred_element_type=F32)
    m_ref[...] = m_new


def _attn_kernel(qmin_ref, qmax_ref, kmin_ref, kmax_ref,
                 tab_smem, tab_rows, posq_ref, posk_ref,
                 lq1_ref, lk1_ref, lq2_ref, lk2_ref, g_ref,
                 q1_ref, q2_ref, k1_ref, k2_ref, v_ref,
                 o_ref,
                 m1_ref, l1_ref, acc1_ref, m2_ref, l2_ref, acc2_ref):
    h = pl.program_id(0)
    qi = pl.program_id(1)
    ki = pl.program_id(2)

    @pl.when(ki == 0)
    def _():
        m1_ref[...] = jnp.full_like(m1_ref, -jnp.inf)
        m2_ref[...] = jnp.full_like(m2_ref, -jnp.inf)
        l1_ref[...] = jnp.zeros_like(l1_ref)
        l2_ref[...] = jnp.zeros_like(l2_ref)
        acc1_ref[...] = jnp.zeros_like(acc1_ref)
        acc2_ref[...] = jnp.zeros_like(acc2_ref)

    nt = (((1,), (1,)), ((), ()))
    s1 = lax.dot_general(q1_ref[...], k1_ref[...], nt, preferred_element_type=F32)
    s2 = lax.dot_general(q2_ref[...], k2_ref[...], nt, preferred_element_type=F32)
    v = v_ref[...]

    all_pos = kmin_ref[ki] - qmax_ref[qi] >= REL_MAX_DIST
    all_neg = kmax_ref[ki] - qmin_ref[qi] <= -REL_MAX_DIST
    nb = REL_BUCKETS // 2

    @pl.when(jnp.logical_or(all_pos, all_neg))
    def _():
        bias = jnp.where(all_pos, tab_smem[REL_BUCKETS - 1, h], tab_smem[nb - 1, h])
        _softmax_update(s1, bias, v, m1_ref, l1_ref, acc1_ref)
        _softmax_update(s2, bias, v, m2_ref, l2_ref, acc2_ref)

    @pl.when(jnp.logical_not(jnp.logical_or(all_pos, all_neg)))
    def _():
        tq, tk = s1.shape
        bucket = _rel_bucket(posk_ref[...] - posq_ref[...])
        row = jnp.broadcast_to(tab_rows[pl.ds(h, 1), :], (tq, LANES))
        bias = jnp.concatenate(
            [jnp.take_along_axis(row, bucket[:, c:c + LANES], axis=1)
             for c in range(0, tk, LANES)], axis=1)
        _softmax_update(s1, bias, v, m1_ref, l1_ref, acc1_ref)
        _softmax_update(s2, bias, v, m2_ref, l2_ref, acc2_ref)

    @pl.when(ki == pl.num_programs(2) - 1)
    def _():
        lam = (jnp.exp(jnp.sum(lq1_ref[...] * lk1_ref[...], axis=-1, keepdims=True))
               - jnp.exp(jnp.sum(lq2_ref[...] * lk2_ref[...], axis=-1, keepdims=True))
               + LAMBDA_INIT)
        o = acc1_ref[...] / l1_ref[...] - lam * (acc2_ref[...] / l2_ref[...])
        o = _rms(o, g_ref[...]) * (1.0 - LAMBDA_INIT)
        o_ref[...] = o.astype(o_ref.dtype)


def _diff_attention(qkv, positions, rel_table, lq1, lk1, lq2, lk2, subln_g, tq, tk):
    S = qkv.shape[0]
    H = N_HEADS
    nq, nk = S // tq, S // tk
    pos = positions.reshape(S).astype(jnp.int32)
    qmin = pos.reshape(nq, tq).min(axis=1)
    qmax = pos.reshape(nq, tq).max(axis=1)
    kmin = pos.reshape(nk, tk).min(axis=1)
    kmax = pos.reshape(nk, tk).max(axis=1)
    tab_rows = jnp.zeros((H, LANES), F32).at[:, :REL_BUCKETS].set(rel_table.T)

    kb = (H * 2 * HEAD_DIM) // HEAD_DIM
    vb = (2 * H * 2 * HEAD_DIM) // V_DIM
    vec = lambda n: pl.BlockSpec((1, n), lambda h, qi, ki, *_: (0, 0))
    in_specs = [
        pl.BlockSpec(memory_space=pltpu.SMEM),
        pl.BlockSpec((H, LANES), lambda h, qi, ki, *_: (0, 0)),
        pl.BlockSpec((tq, 1), lambda h, qi, ki, *_: (qi, 0)),
        pl.BlockSpec((1, tk), lambda h, qi, ki, *_: (0, ki)),
        vec(HEAD_DIM), vec(HEAD_DIM), vec(HEAD_DIM), vec(HEAD_DIM), vec(V_DIM),
        pl.BlockSpec((tq, HEAD_DIM), lambda h, qi, ki, *_: (qi, 2 * h)),
        pl.BlockSpec((tq, HEAD_DIM), lambda h, qi, ki, *_: (qi, 2 * h + 1)),
        pl.BlockSpec((tk, HEAD_DIM), lambda h, qi, ki, *_: (ki, kb + 2 * h)),
        pl.BlockSpec((tk, HEAD_DIM), lambda h, qi, ki, *_: (ki, kb + 2 * h + 1)),
        pl.BlockSpec((tk, V_DIM), lambda h, qi, ki, *_: (ki, vb + h)),
    ]
    grid_spec = pltpu.PrefetchScalarGridSpec(
        num_scalar_prefetch=4,
        grid=(H, nq, nk),
        in_specs=in_specs,
        out_specs=pl.BlockSpec((tq, V_DIM), lambda h, qi, ki, *_: (qi, h)),
        scratch_shapes=[pltpu.VMEM((tq, 1), F32), pltpu.VMEM((tq, 1), F32),
                        pltpu.VMEM((tq, V_DIM), F32),
                        pltpu.VMEM((tq, 1), F32), pltpu.VMEM((tq, 1), F32),
                        pltpu.VMEM((tq, V_DIM), F32)],
    )
    return pl.pallas_call(
        _attn_kernel,
        grid_spec=grid_spec,
        out_shape=jax.ShapeDtypeStruct((S, H * V_DIM), BF16),
        compiler_params=_cparams(("parallel", "parallel", "arbitrary")),
        name="diff_attn",
    )(qmin, qmax, kmin, kmax,
      rel_table, tab_rows, pos.reshape(S, 1), pos.reshape(1, S),
      lq1.reshape(1, -1), lk1.reshape(1, -1), lq2.reshape(1, -1), lk2.reshape(1, -1),
      subln_g.reshape(1, -1),
      qkv, qkv, qkv, qkv, qkv)


HALO = 2 * SUBLANES
CONV_COLS_CHUNK = 2 * LANES
CONV_ROWS_CHUNK = 64


def _conv_kernel(prev_ref, cur_ref, next_ref, w_ref, b_ref, g_ref, beta_ref,
                 o_ref, buf_ref, conv_ref):
    i = pl.program_id(0)
    ts, C = cur_ref.shape
    buf_ref[0:HALO, :] = jnp.where(i > 0, prev_ref[...], 0.0)
    buf_ref[HALO:HALO + ts, :] = cur_ref[...]
    buf_ref[HALO + ts:, :] = jnp.where(i < pl.num_programs(0) - 1, next_ref[...], 0.0)

    def col_body(cc, carry):
        c0 = pl.multiple_of(cc * CONV_COLS_CHUNK, CONV_COLS_CHUNK)
        cols = pl.ds(c0, CONV_COLS_CHUNK)
        for r0 in range(0, ts, CONV_ROWS_CHUNK):
            acc = jnp.broadcast_to(b_ref[:, cols], (CONV_ROWS_CHUNK, CONV_COLS_CHUNK))
            for j in range(CONV_WIDTH):
                start = HALO - CONV_PAD + j + r0
                acc = acc + w_ref[j:j + 1, cols] * buf_ref[start:start + CONV_ROWS_CHUNK, cols]
            conv_ref[r0:r0 + CONV_ROWS_CHUNK, cols] = acc
        return carry

    lax.fori_loop(0, C // CONV_COLS_CHUNK, col_body, 0)

    c = conv_ref[...]
    mu = jnp.mean(c, axis=-1, keepdims=True)
    d = c - mu
    var = jnp.mean(d * d, axis=-1, keepdims=True)
    y = d * lax.rsqrt(var + EPS) * g_ref[...] + beta_ref[...]
    o_ref[...] = (y * _sigmoid(y)).astype(o_ref.dtype)


def _conv_module(c, w_dw, b_dw, ln_g, ln_b, ts=256):
    S, C = c.shape
    hb = ts // HALO
    last = S // HALO - 1
    vec = pl.BlockSpec((1, C), lambda i: (0, 0))
    return pl.pallas_call(
        _conv_kernel,
        grid=(S // ts,),
        in_specs=[
            pl.BlockSpec((HALO, C), lambda i: (jnp.maximum(i * hb - 1, 0), 0)),
            pl.BlockSpec((ts, C), lambda i: (i, 0)),
            pl.BlockSpec((HALO, C), lambda i: (jnp.minimum((i + 1) * hb, last), 0)),
            pl.BlockSpec((CONV_WIDTH, C), lambda i: (0, 0)),
            vec, vec, vec,
        ],
        out_specs=pl.BlockSpec((ts, C), lambda i: (i, 0)),
        out_shape=jax.ShapeDtypeStruct((S, C), BF16),
        scratch_shapes=[pltpu.VMEM((ts + 2 * HALO, C), F32),
                        pltpu.VMEM((ts, C), F32)],
        compiler_params=_cparams(("parallel",)),
        name="conv_module",
    )(c, c, c, w_dw.reshape(CONV_WIDTH, C), b_dw.reshape(1, C),
      ln_g.reshape(1, C), ln_b.reshape(1, C))


def kernel(x, p, positions, rel_table, mix_pre_g, w_in, lambda_q1, lambda_k1, lambda_q2, lambda_k2, subln_g, w_attn_o, w_dw, b_dw, conv_ln_g, conv_ln_b, w_conv_o, w_out, mix_post_g, ffn_pre_g, w_up, w_down, ffn_post_g, w_ple_gate, w_ple_proj, ple_post_g):
    B, S, D = x.shape
    assert B == 1 and w_in.shape[0] == 1
    x2d = x.reshape(S, D)
    qkv_cols = 3 * N_HEADS * 2 * HEAD_DIM
    conv_off = qkv_cols
    gate_off = qkv_cols + 2 * D

    w_in_b = w_in[0].astype(BF16)
    w_attn_o_b = w_attn_o[0].astype(BF16)
    w_conv_o_b = w_conv_o[0].astype(BF16)
    w_out_b = w_out[0].astype(BF16)
    w_up_b = w_up[0].astype(BF16)
    w_down_b = w_down[0].astype(BF16)
    w_ple_gate_b = w_ple_gate[0].astype(BF16)
    w_ple_proj_b = w_ple_proj[0].astype(BF16)
    p_b = p[0, 0].astype(BF16)

    tm, tn = 1024, 512

    h = _prenorm(x2d, mix_pre_g[0])
    qkv = _fused_mm([(h, w_in_b, 0)], [], lambda a: a, qkv_cols, BF16, tm, tn, "qkv_proj")
    glu = _fused_mm([(h, w_in_b, conv_off), (h, w_in_b, conv_off + D)], [],
                    lambda a, b: a * _sigmoid(b), D, F32, tm, tn, "conv_glu")
    gates = _fused_mm([(h, w_in_b, gate_off)], [], _sigmoid, 2 * D, F32, tm, tn, "merge_gates")

    o = _diff_attention(qkv, positions, rel_table, lambda_q1[0], lambda_k1[0],
                        lambda_q2[0], lambda_k2[0], subln_g[0], tq=512, tk=512)
    c = _conv_module(glu, w_dw[0], b_dw[0], conv_ln_g[0], conv_ln_b[0])

    merged = _fused_mm([(o, w_attn_o_b, 0), (c, w_conv_o_b, 0)],
                       [(gates, 0), (gates, D)],
                       lambda ya, yc, ga, gc: ga * ya + gc * yc,
                       D, BF16, tm // 2, tn, "branch_merge")
    mix = _fused_mm([(merged, w_out_b, 0)], [], lambda a: a, D, F32, tm, tn, "mix_out")
    x1, h2 = _postnorm_res(x2d, mix, mix_post_g[0], ffn_pre_g[0])

    u = _fused_mm([(h2, w_up_b, 0)], [], lambda a: jnp.square(jnp.maximum(a, 0.0)),
                  w_up_b.shape[1], BF16, tm, tn, "ffn_up")
    dn = _mm_ktiled(u, w_down_b, 1024, 1024, 2048, "ffn_down")
    x2, x2_b = _postnorm_res(x1, dn, ffn_post_g[0])

    ple = _fused_mm([(x2_b, w_ple_gate_b, 0), (p_b, w_ple_proj_b, 0)], [],
                    lambda a, e: _sigmoid(a) * e, D, F32, tm, tn, "ple_gate")
    out = _final_res(x2, ple, ple_post_g[0])
    return out.reshape(B, S, D)
```

```python
import functools
import math

import jax
import jax.numpy as jnp
from jax import lax
from jax.experimental import pallas as pl
from jax.experimental.pallas import tpu as pltpu

N_HEADS = 16
HEAD_DIM = 128
V_DIM = 2 * HEAD_DIM
CONV_WIDTH = 31
CONV_PAD = CONV_WIDTH // 2
REL_BUCKETS = 32
REL_MAX_DIST = 128
EPS = 1e-6
LAMBDA_INIT = 0.8 - 0.6 * math.exp(-0.3 * 0)

LANES = 128
SUBLANES = 8
VMEM_LIMIT_BYTES = 56 * 1024 * 1024

F32 = jnp.float32
BF16 = jnp.bfloat16


def _cparams(sem, flags=None):
    return pltpu.CompilerParams(dimension_semantics=sem,
                                vmem_limit_bytes=VMEM_LIMIT_BYTES, flags=flags)


def _sigmoid(x):
    return 1.0 / (1.0 + jnp.exp(-x))


def _rms(x, g):
    return x * lax.rsqrt(jnp.mean(x * x, axis=-1, keepdims=True) + EPS) * g


def _prenorm_kernel(x_ref, g_ref, h_ref):
    h_ref[...] = _rms(x_ref[...], g_ref[...]).astype(h_ref.dtype)


def _prenorm(x, g, tr=256):
    S, D = x.shape
    return pl.pallas_call(
        _prenorm_kernel,
        grid=(S // tr,),
        in_specs=[pl.BlockSpec((tr, D), lambda i: (i, 0)),
                  pl.BlockSpec((1, D), lambda i: (0, 0))],
        out_specs=pl.BlockSpec((tr, D), lambda i: (i, 0)),
        out_shape=jax.ShapeDtypeStruct((S, D), BF16),
        compiler_params=_cparams(("parallel",)),
        name="prenorm",
    )(x, g.reshape(1, D))


def _postnorm_res_kernel(x_ref, y_ref, gpost_ref, *rest, with_prenorm):
    x_new = x_ref[...] + _rms(y_ref[...], gpost_ref[...])
    if with_prenorm:
        gpre_ref, xo_ref, h_ref = rest
        xo_ref[...] = x_new
        h_ref[...] = _rms(x_new, gpre_ref[...]).astype(h_ref.dtype)
    else:
        xo_ref, h_ref = rest
        xo_ref[...] = x_new
        h_ref[...] = x_new.astype(h_ref.dtype)


def _postnorm_res(x, y, g_post, g_pre=None, tr=256):
    S, D = x.shape
    row = pl.BlockSpec((tr, D), lambda i: (i, 0))
    vec = pl.BlockSpec((1, D), lambda i: (0, 0))
    args = [x, y, g_post.reshape(1, D)]
    in_specs = [row, row, vec]
    if g_pre is not None:
        args.append(g_pre.reshape(1, D))
        in_specs.append(vec)
    return pl.pallas_call(
        functools.partial(_postnorm_res_kernel, with_prenorm=g_pre is not None),
        grid=(S // tr,),
        in_specs=in_specs,
        out_specs=[row, row],
        out_shape=[jax.ShapeDtypeStruct((S, D), F32),
                   jax.ShapeDtypeStruct((S, D), BF16)],
        compiler_params=_cparams(("parallel",)),
        name="postnorm_res",
    )(*args)


def _final_kernel(x_ref, y_ref, g_ref, o_ref):
    o_ref[...] = x_ref[...] + _rms(y_ref[...], g_ref[...])


def _final_res(x, y, g, tr=256):
    S, D = x.shape
    row = pl.BlockSpec((tr, D), lambda i: (i, 0))
    return pl.pallas_call(
        _final_kernel,
        grid=(S // tr,),
        in_specs=[row, row, pl.BlockSpec((1, D), lambda i: (0, 0))],
        out_specs=row,
        out_shape=jax.ShapeDtypeStruct((S, D), F32),
        compiler_params=_cparams(("parallel",)),
        name="final_res",
    )(x, y, g.reshape(1, D))


def _fused_mm_kernel(*refs, n_pairs, epilogue):
    out_ref = refs[-1]
    accs = [jnp.dot(refs[2 * i][...], refs[2 * i + 1][...],
                    preferred_element_type=F32) for i in range(n_pairs)]
    extras = [r[...] for r in refs[2 * n_pairs:-1]]
    out_ref[...] = epilogue(*accs, *extras).astype(out_ref.dtype)


def _fused_mm(pairs, extras, epilogue, n_out, out_dtype, tm, tn, name):
    M = pairs[0][0].shape[0]
    in_specs, args = [], []
    for a, w, off in pairs:
        K = a.shape[1]
        ob = off // tn
        in_specs.append(pl.BlockSpec((tm, K), lambda m, n: (m, 0)))
        in_specs.append(pl.BlockSpec((K, tn), lambda m, n, ob=ob: (0, n + ob)))
        args += [a, w]
    for e, off in extras:
        ob = off // tn
        in_specs.append(pl.BlockSpec((tm, tn), lambda m, n, ob=ob: (m, n + ob)))
        args.append(e)
    return pl.pallas_call(
        functools.partial(_fused_mm_kernel, n_pairs=len(pairs), epilogue=epilogue),
        grid=(M // tm, n_out // tn),
        in_specs=in_specs,
        out_specs=pl.BlockSpec((tm, tn), lambda m, n: (m, n)),
        out_shape=jax.ShapeDtypeStruct((M, n_out), out_dtype),
        compiler_params=_cparams(("parallel", "arbitrary")),
        name=name,
    )(*args)


def _mm_ktiled_kernel(a_ref, w_ref, o_ref, acc_ref):
    k = pl.program_id(2)

    @pl.when(k == 0)
    def _():
        acc_ref[...] = jnp.zeros_like(acc_ref)

    acc_ref[...] += jnp.dot(a_ref[...], w_ref[...], preferred_element_type=F32)

    @pl.when(k == pl.num_programs(2) - 1)
    def _():
        o_ref[...] = acc_ref[...]


def _mm_ktiled(a, w, tm, tn, tk, name):
    M, K = a.shape
    N = w.shape[1]
    return pl.pallas_call(
        _mm_ktiled_kernel,
        grid=(M // tm, N // tn, K // tk),
        in_specs=[pl.BlockSpec((tm, tk), lambda m, n, k: (m, k)),
                  pl.BlockSpec((tk, tn), lambda m, n, k: (k, n))],
        out_specs=pl.BlockSpec((tm, tn), lambda m, n, k: (m, n)),
        out_shape=jax.ShapeDtypeStruct((M, N), F32),
        scratch_shapes=[pltpu.VMEM((tm, tn), F32)],
        compiler_params=_cparams(("parallel", "parallel", "arbitrary")),
        name=name,
    )(a, w)


def _rel_bucket(rel):
    nb = REL_BUCKETS // 2
    max_exact = nb // 2
    ret = jnp.where(rel > 0, nb, 0)
    n = jnp.abs(rel)
    nf = jnp.maximum(n, 1).astype(F32)
    large = max_exact + (jnp.log(nf / max_exact) / math.log(REL_MAX_DIST / max_exact)
                         * (nb - max_exact)).astype(jnp.int32)
    large = jnp.minimum(large, nb - 1)
    return ret + jnp.where(n < max_exact, n, large)


LOG2E = math.log2(math.e)
LOGIT_SCALE_LOG2 = (HEAD_DIM ** -0.5) * LOG2E
ATTN_KEYS_CHUNK = 256

def _attn_logits(maps, s_scrs, r_scrs):
    logits = [lax.dot_general(q_ref[...], k_ref[...], (((1,), (1,)), ((), ())),
                              preferred_element_type=F32)
              for q_ref, k_ref, _, _, _ in maps]
    for s, s_scr, r_scr in zip(logits, s_scrs, r_scrs):
        s_scr[...] = s
        r_scr[...] = jnp.max(s, axis=-1, keepdims=True)


def _attn_accumulate(maps, s_scrs, r_scrs, v_ref, bias_l):
    tk = v_ref.shape[0]
    rows = []
    for (_, _, m_ref, _, _), s_scr, r_scr in zip(maps, s_scrs, r_scrs):
        m_prev = m_ref[...]
        if bias_l.ndim == 0:
            m_new = jnp.maximum(m_prev, r_scr[...] * LOGIT_SCALE_LOG2 + bias_l)
            shift = bias_l - m_new
        else:
            t = s_scr[...] * LOGIT_SCALE_LOG2 + bias_l
            m_new = jnp.maximum(m_prev, jnp.max(t, axis=-1, keepdims=True))
            s_scr[...] = t - m_new
            shift = None
        m_ref[...] = m_new
        rows.append((jnp.exp2(m_prev - m_new), shift))
    sums = [None, None]
    probs = [[], []]
    for c in range(0, tk, ATTN_KEYS_CHUNK):
        for i, ((_, shift), s_scr) in enumerate(zip(rows, s_scrs)):
            s = s_scr[:, c:c + ATTN_KEYS_CHUNK]
            p = jnp.exp2(s if shift is None else s * LOGIT_SCALE_LOG2 + shift)
            psum = jnp.sum(p, axis=-1, keepdims=True)
            sums[i] = psum if sums[i] is None else sums[i] + psum
            probs[i].append(p.astype(v_ref.dtype))
    for (alpha, _), psum, blocks, (_, _, _, l_ref, acc_ref) in zip(rows, sums, probs, maps):
        l_ref[...] = alpha * l_ref[...] + psum
        acc_ref[...] = alpha * acc_ref[...] + jnp.dot(
            jnp.concatenate(blocks, axis=1), v_ref[...], preferred_element_type=F32)


def _attn_kernel(qmin_ref, qmax_ref, kmin_ref, kmax_ref,
                 tab_smem, tab_rows, posq_ref, posk_ref,
                 lq1_ref, lk1_ref, lq2_ref, lk2_ref, g_ref,
                 q1_ref, q2_ref, k1_ref, k2_ref, v_ref,
                 o_ref,
                 m1_ref, l1_ref, acc1_ref, m2_ref, l2_ref, acc2_ref,
                 s1a_scr, s2a_scr, r1a_scr, r2a_scr,
                 s1b_scr, s2b_scr, r1b_scr, r2b_scr, *, nk):
    h = pl.program_id(0)
    qi = pl.program_id(1)
    j = pl.program_id(2)
    maps = [(q1_ref, k1_ref, m1_ref, l1_ref, acc1_ref),
            (q2_ref, k2_ref, m2_ref, l2_ref, acc2_ref)]
    slots = [((s1a_scr, s2a_scr), (r1a_scr, r2a_scr)),
             ((s1b_scr, s2b_scr), (r1b_scr, r2b_scr))]

    kt = jnp.maximum(j - 1, 0)
    all_pos = kmin_ref[kt] - qmax_ref[qi] >= REL_MAX_DIST
    all_neg = kmax_ref[kt] - qmin_ref[qi] <= -REL_MAX_DIST
    const_bias = jnp.logical_or(all_pos, all_neg)
    nb = REL_BUCKETS // 2

    def bias_scalar():
        return LOG2E * jnp.where(all_pos, tab_smem[REL_BUCKETS - 1, h], tab_smem[nb - 1, h])

    def bias_tile():
        tq, tk = q1_ref.shape[0], k1_ref.shape[0]
        bucket = _rel_bucket(posk_ref[...] - posq_ref[...])
        row = jnp.broadcast_to(tab_rows[pl.ds(h, 1), :], (tq, LANES)) * LOG2E
        return jnp.concatenate(
            [jnp.take_along_axis(row, bucket[:, c:c + LANES], axis=1)
             for c in range(0, tk, LANES)], axis=1)

    def finalize():
        lam = (jnp.exp(jnp.sum(lq1_ref[...] * lk1_ref[...], axis=-1, keepdims=True))
               - jnp.exp(jnp.sum(lq2_ref[...] * lk2_ref[...], axis=-1, keepdims=True))
               + LAMBDA_INIT)
        o = acc1_ref[...] / l1_ref[...] - lam * (acc2_ref[...] / l2_ref[...])
        o = _rms(o, g_ref[...]) * (1.0 - LAMBDA_INIT)
        o_ref[...] = o.astype(o_ref.dtype)

    @pl.when(j == 0)
    def _():
        m1_ref[...] = jnp.full_like(m1_ref, -jnp.inf)
        m2_ref[...] = jnp.full_like(m2_ref, -jnp.inf)
        l1_ref[...] = jnp.zeros_like(l1_ref)
        l2_ref[...] = jnp.zeros_like(l2_ref)
        acc1_ref[...] = jnp.zeros_like(acc1_ref)
        acc2_ref[...] = jnp.zeros_like(acc2_ref)
        _attn_logits(maps, *slots[0])

    bias_kinds = ((const_bias, bias_scalar), (jnp.logical_not(const_bias), bias_tile))
    middle = jnp.logical_and(j > 0, j < nk)
    for parity in (0, 1):
        on_parity = jnp.logical_and(middle, j % 2 == parity)
        for is_kind, get_bias in bias_kinds:
            @pl.when(jnp.logical_and(on_parity, is_kind))
            def _(parity=parity, get_bias=get_bias):
                _attn_logits(maps, *slots[parity])
                _attn_accumulate(maps, *slots[1 - parity], v_ref, get_bias())

    for is_kind, get_bias in bias_kinds:
        @pl.when(jnp.logical_and(j == nk, is_kind))
        def _(get_bias=get_bias):
            _attn_accumulate(maps, *slots[(nk - 1) % 2], v_ref, get_bias())
            finalize()


def _diff_attention(qkv, positions, rel_table, lq1, lk1, lq2, lk2, subln_g, tq, tk):
    S = qkv.shape[0]
    H = N_HEADS
    nq, nk = S // tq, S // tk
    pos = positions.reshape(S).astype(jnp.int32)
    qmin = pos.reshape(nq, tq).min(axis=1)
    qmax = pos.reshape(nq, tq).max(axis=1)
    kmin = pos.reshape(nk, tk).min(axis=1)
    kmax = pos.reshape(nk, tk).max(axis=1)
    tab_rows = jnp.zeros((H, LANES), F32).at[:, :REL_BUCKETS].set(rel_table.T)

    kb = (H * 2 * HEAD_DIM) // HEAD_DIM
    vb = (2 * H * 2 * HEAD_DIM) // V_DIM
    vec = lambda n: pl.BlockSpec((1, n), lambda h, qi, j, *_: (0, 0))
    cur = lambda j: jnp.minimum(j, nk - 1)
    prev = lambda j: jnp.maximum(j - 1, 0)
    in_specs = [
        pl.BlockSpec(memory_space=pltpu.SMEM),
        pl.BlockSpec((H, LANES), lambda h, qi, j, *_: (0, 0)),
        pl.BlockSpec((tq, 1), lambda h, qi, j, *_: (qi, 0)),
        pl.BlockSpec((1, tk), lambda h, qi, j, *_: (0, prev(j))),
        vec(HEAD_DIM), vec(HEAD_DIM), vec(HEAD_DIM), vec(HEAD_DIM), vec(V_DIM),
        pl.BlockSpec((tq, HEAD_DIM), lambda h, qi, j, *_: (qi, 2 * h)),
        pl.BlockSpec((tq, HEAD_DIM), lambda h, qi, j, *_: (qi, 2 * h + 1)),
        pl.BlockSpec((tk, HEAD_DIM), lambda h, qi, j, *_: (cur(j), kb + 2 * h)),
        pl.BlockSpec((tk, HEAD_DIM), lambda h, qi, j, *_: (cur(j), kb + 2 * h + 1)),
        pl.BlockSpec((tk, V_DIM), lambda h, qi, j, *_: (prev(j), vb + h)),
    ]
    state = [pltpu.VMEM((tq, 1), F32), pltpu.VMEM((tq, 1), F32), pltpu.VMEM((tq, V_DIM), F32)]
    slot = [pltpu.VMEM((tq, tk), F32), pltpu.VMEM((tq, tk), F32),
            pltpu.VMEM((tq, 1), F32), pltpu.VMEM((tq, 1), F32)]
    grid_spec = pltpu.PrefetchScalarGridSpec(
        num_scalar_prefetch=4,
        grid=(H, nq, nk + 1),
        in_specs=in_specs,
        out_specs=pl.BlockSpec((tq, V_DIM), lambda h, qi, j, *_: (qi, h)),
        scratch_shapes=state + state + slot + slot,
    )
    return pl.pallas_call(
        functools.partial(_attn_kernel, nk=nk),
        grid_spec=grid_spec,
        out_shape=jax.ShapeDtypeStruct((S, H * V_DIM), BF16),
        compiler_params=_cparams(("parallel", "parallel", "arbitrary")),
        name="diff_attn",
    )(qmin, qmax, kmin, kmax,
      rel_table, tab_rows, pos.reshape(S, 1), pos.reshape(1, S),
      lq1.reshape(1, -1), lk1.reshape(1, -1), lq2.reshape(1, -1), lk2.reshape(1, -1),
      subln_g.reshape(1, -1),
      qkv, qkv, qkv, qkv, qkv)


HALO = 2 * SUBLANES
CONV_COLS_CHUNK = 2 * LANES
CONV_ROWS_CHUNK = 64


def _conv_kernel(prev_ref, cur_ref, next_ref, w_ref, b_ref, g_ref, beta_ref,
                 o_ref, buf_ref, conv_ref):
    i = pl.program_id(0)
    ts, C = cur_ref.shape
    buf_ref[0:HALO, :] = jnp.where(i > 0, prev_ref[...], 0.0)
    buf_ref[HALO:HALO + ts, :] = cur_ref[...]
    buf_ref[HALO + ts:, :] = jnp.where(i < pl.num_programs(0) - 1, next_ref[...], 0.0)

    def col_body(cc, carry):
        c0 = pl.multiple_of(cc * CONV_COLS_CHUNK, CONV_COLS_CHUNK)
        cols = pl.ds(c0, CONV_COLS_CHUNK)
        for r0 in range(0, ts, CONV_ROWS_CHUNK):
            acc = jnp.broadcast_to(b_ref[:, cols], (CONV_ROWS_CHUNK, CONV_COLS_CHUNK))
            for j in range(CONV_WIDTH):
                start = HALO - CONV_PAD + j + r0
                acc = acc + w_ref[j:j + 1, cols] * buf_ref[start:start + CONV_ROWS_CHUNK, cols]
            conv_ref[r0:r0 + CONV_ROWS_CHUNK, cols] = acc
        return carry

    lax.fori_loop(0, C // CONV_COLS_CHUNK, col_body, 0)

    c = conv_ref[...]
    mu = jnp.mean(c, axis=-1, keepdims=True)
    d = c - mu
    var = jnp.mean(d * d, axis=-1, keepdims=True)
    y = d * lax.rsqrt(var + EPS) * g_ref[...] + beta_ref[...]
    o_ref[...] = (y * _sigmoid(y)).astype(o_ref.dtype)


def _conv_module(c, w_dw, b_dw, ln_g, ln_b, ts=256):
    S, C = c.shape
    hb = ts // HALO
    last = S // HALO - 1
    vec = pl.BlockSpec((1, C), lambda i: (0, 0))
    return pl.pallas_call(
        _conv_kernel,
        grid=(S // ts,),
        in_specs=[
            pl.BlockSpec((HALO, C), lambda i: (jnp.maximum(i * hb - 1, 0), 0)),
            pl.BlockSpec((ts, C), lambda i: (i, 0)),
            pl.BlockSpec((HALO, C), lambda i: (jnp.minimum((i + 1) * hb, last), 0)),
            pl.BlockSpec((CONV_WIDTH, C), lambda i: (0, 0)),
            vec, vec, vec,
        ],
        out_specs=pl.BlockSpec((ts, C), lambda i: (i, 0)),
        out_shape=jax.ShapeDtypeStruct((S, C), BF16),
        scratch_shapes=[pltpu.VMEM((ts + 2 * HALO, C), F32),
                        pltpu.VMEM((ts, C), F32)],
        compiler_params=_cparams(("parallel",)),
        name="conv_module",
    )(c, c, c, w_dw.reshape(CONV_WIDTH, C), b_dw.reshape(1, C),
      ln_g.reshape(1, C), ln_b.reshape(1, C))


def kernel(x, p, positions, rel_table, mix_pre_g, w_in, lambda_q1, lambda_k1, lambda_q2, lambda_k2, subln_g, w_attn_o, w_dw, b_dw, conv_ln_g, conv_ln_b, w_conv_o, w_out, mix_post_g, ffn_pre_g, w_up, w_down, ffn_post_g, w_ple_gate, w_ple_proj, ple_post_g):
    B, S, D = x.shape
    assert B == 1 and w_in.shape[0] == 1
    x2d = x.reshape(S, D)
    qkv_cols = 3 * N_HEADS * 2 * HEAD_DIM
    conv_off = qkv_cols
    gate_off = qkv_cols + 2 * D

    w_in_b = w_in[0].astype(BF16)
    w_attn_o_b = w_attn_o[0].astype(BF16)
    w_conv_o_b = w_conv_o[0].astype(BF16)
    w_out_b = w_out[0].astype(BF16)
    w_up_b = w_up[0].astype(BF16)
    w_down_b = w_down[0].astype(BF16)
    w_ple_gate_b = w_ple_gate[0].astype(BF16)
    w_ple_proj_b = w_ple_proj[0].astype(BF16)
    p_b = p[0, 0].astype(BF16)

    tm, tn = 1024, 512

    h = _prenorm(x2d, mix_pre_g[0])
    qkv = _fused_mm([(h, w_in_b, 0)], [], lambda a: a, qkv_cols, BF16, tm, tn, "qkv_proj")
    glu = _fused_mm([(h, w_in_b, conv_off), (h, w_in_b, conv_off + D)], [],
                    lambda a, b: a * _sigmoid(b), D, F32, tm, tn, "conv_glu")
    gates = _fused_mm([(h, w_in_b, gate_off)], [], _sigmoid, 2 * D, F32, tm, tn, "merge_gates")

    o = _diff_attention(qkv, positions, rel_table, lambda_q1[0], lambda_k1[0],
                        lambda_q2[0], lambda_k2[0], subln_g[0], tq=512, tk=1024)
    c = _conv_module(glu, w_dw[0], b_dw[0], conv_ln_g[0], conv_ln_b[0])

    merged = _fused_mm([(o, w_attn_o_b, 0), (c, w_conv_o_b, 0)],
                       [(gates, 0), (gates, D)],
                       lambda ya, yc, ga, gc: ga * ya + gc * yc,
                       D, BF16, tm // 2, tn, "branch_merge")
    mix = _fused_mm([(merged, w_out_b, 0)], [], lambda a: a, D, F32, tm, tn, "mix_out")
    x1, h2 = _postnorm_res(x2d, mix, mix_post_g[0], ffn_pre_g[0])

    u = _fused_mm([(h2, w_up_b, 0)], [], lambda a: jnp.square(jnp.maximum(a, 0.0)),
                  w_up_b.shape[1], BF16, tm, tn, "ffn_up")
    dn = _mm_ktiled(u, w_down_b, 1024, 1024, 2048, "ffn_down")
    x2, x2_b = _postnorm_res(x1, dn, ffn_post_g[0])

    ple = _fused_mm([(x2_b, w_ple_gate_b, 0), (p_b, w_ple_proj_b, 0)], [],
                    lambda a, e: _sigmoid(a) * e, D, F32, tm, tn, "ple_gate")
    out = _final_res(x2, ple, ple_post_g[0])
    return out.reshape(B, S, D)
```

```python
import functools
import math

import jax
import jax.numpy as jnp
from jax import lax
from jax.experimental import pallas as pl
from jax.experimental.pallas import tpu as pltpu

N_HEADS = 16
HEAD_DIM = 128
V_DIM = 2 * HEAD_DIM
CONV_WIDTH = 31
CONV_PAD = CONV_WIDTH // 2
REL_BUCKETS = 32
REL_MAX_DIST = 128
EPS = 1e-6
LAMBDA_INIT = 0.8 - 0.6 * math.exp(-0.3 * 0)

LANES = 128
SUBLANES = 8
VMEM_LIMIT_BYTES = 56 * 1024 * 1024

F32 = jnp.float32
BF16 = jnp.bfloat16


def _cparams(sem, flags=None):
    return pltpu.CompilerParams(dimension_semantics=sem,
                                vmem_limit_bytes=VMEM_LIMIT_BYTES, flags=flags)


def _sigmoid(x):
    return 1.0 / (1.0 + jnp.exp(-x))


def _rms(x, g):
    return x * lax.rsqrt(jnp.mean(x * x, axis=-1, keepdims=True) + EPS) * g


def _prenorm_kernel(x_ref, g_ref, h_ref):
    h_ref[...] = _rms(x_ref[...], g_ref[...]).astype(h_ref.dtype)


def _prenorm(x, g, tr=256):
    S, D = x.shape
    return pl.pallas_call(
        _prenorm_kernel,
        grid=(S // tr,),
        in_specs=[pl.BlockSpec((tr, D), lambda i: (i, 0)),
                  pl.BlockSpec((1, D), lambda i: (0, 0))],
        out_specs=pl.BlockSpec((tr, D), lambda i: (i, 0)),
        out_shape=jax.ShapeDtypeStruct((S, D), BF16),
        compiler_params=_cparams(("parallel",)),
        name="prenorm",
    )(x, g.reshape(1, D))


def _postnorm_res_kernel(x_ref, y_ref, gpost_ref, *rest, with_prenorm):
    x_new = x_ref[...] + _rms(y_ref[...], gpost_ref[...])
    if with_prenorm:
        gpre_ref, xo_ref, h_ref = rest
        xo_ref[...] = x_new
        h_ref[...] = _rms(x_new, gpre_ref[...]).astype(h_ref.dtype)
    else:
        xo_ref, h_ref = rest
        xo_ref[...] = x_new
        h_ref[...] = x_new.astype(h_ref.dtype)


def _postnorm_res(x, y, g_post, g_pre=None, tr=256):
    S, D = x.shape
    row = pl.BlockSpec((tr, D), lambda i: (i, 0))
    vec = pl.BlockSpec((1, D), lambda i: (0, 0))
    args = [x, y, g_post.reshape(1, D)]
    in_specs = [row, row, vec]
    if g_pre is not None:
        args.append(g_pre.reshape(1, D))
        in_specs.append(vec)
    return pl.pallas_call(
        functools.partial(_postnorm_res_kernel, with_prenorm=g_pre is not None),
        grid=(S // tr,),
        in_specs=in_specs,
        out_specs=[row, row],
        out_shape=[jax.ShapeDtypeStruct((S, D), F32),
                   jax.ShapeDtypeStruct((S, D), BF16)],
        compiler_params=_cparams(("parallel",)),
        name="postnorm_res",
    )(*args)


def _final_kernel(x_ref, y_ref, g_ref, o_ref):
    o_ref[...] = x_ref[...] + _rms(y_ref[...], g_ref[...])


def _final_res(x, y, g, tr=256):
    S, D = x.shape
    row = pl.BlockSpec((tr, D), lambda i: (i, 0))
    return pl.pallas_call(
        _final_kernel,
        grid=(S // tr,),
        in_specs=[row, row, pl.BlockSpec((1, D), lambda i: (0, 0))],
        out_specs=row,
        out_shape=jax.ShapeDtypeStruct((S, D), F32),
        compiler_params=_cparams(("parallel",)),
        name="final_res",
    )(x, y, g.reshape(1, D))


def _fused_mm_kernel(*refs, a_index, epilogue):
    n_a = max(a_index) + 1
    n_pairs = len(a_index)
    out_ref = refs[-1]
    accs = [jnp.dot(refs[a_index[i]][...], refs[n_a + i][...].astype(BF16),
                    preferred_element_type=F32) for i in range(n_pairs)]
    extras = [r[...] for r in refs[n_a + n_pairs:-1]]
    out_ref[...] = epilogue(*accs, *extras).astype(out_ref.dtype)


def _fused_mm(pairs, extras, epilogue, n_out, out_dtype, tm, tn, name, single_buffer_a=False):
    M = pairs[0][0].shape[0]
    a_list, a_index = [], []
    for a, _, _ in pairs:
        for idx, seen in enumerate(a_list):
            if seen is a:
                a_index.append(idx)
                break
        else:
            a_index.append(len(a_list))
            a_list.append(a)
    a_mode = dict(pipeline_mode=pl.Buffered(1)) if single_buffer_a else {}
    in_specs = [pl.BlockSpec((tm, a.shape[1]), lambda m, n: (m, 0), **a_mode) for a in a_list]
    args = list(a_list)
    for a, w, off in pairs:
        ob = off // tn
        in_specs.append(pl.BlockSpec((a.shape[1], tn), lambda m, n, ob=ob: (0, n + ob)))
        args.append(w)
    for e, off in extras:
        ob = off // tn
        in_specs.append(pl.BlockSpec((tm, tn), lambda m, n, ob=ob: (m, n + ob)))
        args.append(e)
    return pl.pallas_call(
        functools.partial(_fused_mm_kernel, a_index=tuple(a_index), epilogue=epilogue),
        grid=(M // tm, n_out // tn),
        in_specs=in_specs,
        out_specs=pl.BlockSpec((tm, tn), lambda m, n: (m, n)),
        out_shape=jax.ShapeDtypeStruct((M, n_out), out_dtype),
        compiler_params=_cparams(("parallel", "arbitrary")),
        name=name,
    )(*args)


def _mm_ktiled_kernel(a_ref, w_ref, o_ref, acc_ref):
    k = pl.program_id(2)

    @pl.when(k == 0)
    def _():
        acc_ref[...] = jnp.zeros_like(acc_ref)

    acc_ref[...] += jnp.dot(a_ref[...], w_ref[...].astype(BF16), preferred_element_type=F32)

    @pl.when(k == pl.num_programs(2) - 1)
    def _():
        o_ref[...] = acc_ref[...]


def _mm_ktiled(a, w, tm, tn, tk, name):
    M, K = a.shape
    N = w.shape[1]
    return pl.pallas_call(
        _mm_ktiled_kernel,
        grid=(M // tm, N // tn, K // tk),
        in_specs=[pl.BlockSpec((tm, tk), lambda m, n, k: (m, k)),
                  pl.BlockSpec((tk, tn), lambda m, n, k: (k, n))],
        out_specs=pl.BlockSpec((tm, tn), lambda m, n, k: (m, n)),
        out_shape=jax.ShapeDtypeStruct((M, N), F32),
        scratch_shapes=[pltpu.VMEM((tm, tn), F32)],
        compiler_params=_cparams(("parallel", "parallel", "arbitrary")),
        name=name,
    )(a, w)


def _rel_bucket(rel):
    nb = REL_BUCKETS // 2
    max_exact = nb // 2
    ret = jnp.where(rel > 0, nb, 0)
    n = jnp.abs(rel)
    nf = jnp.maximum(n, 1).astype(F32)
    large = max_exact + (jnp.log(nf / max_exact) / math.log(REL_MAX_DIST / max_exact)
                         * (nb - max_exact)).astype(jnp.int32)
    large = jnp.minimum(large, nb - 1)
    return ret + jnp.where(n < max_exact, n, large)


LOG2E = math.log2(math.e)
LOGIT_SCALE_LOG2 = (HEAD_DIM ** -0.5) * LOG2E
ATTN_KEYS_CHUNK = 256
ATTN_BIAS_SPREAD_MAX = 64.0

def _attn_logits(maps, s_scrs, r_scrs):
    logits = [lax.dot_general(q_ref[...], k_ref[...], (((1,), (1,)), ((), ())),
                              preferred_element_type=F32)
              for q_ref, k_ref, _, _, _ in maps]
    for s, s_scr, r_scr in zip(logits, s_scrs, r_scrs):
        s_scr[...] = s
        r_scr[...] = jnp.max(s, axis=-1, keepdims=True)


def _attn_accumulate(maps, s_scrs, r_scrs, v_ref, bias_l, bias_block=None):
    tk = v_ref.shape[0]
    rows = []
    for (_, _, m_ref, _, _), s_scr, r_scr in zip(maps, s_scrs, r_scrs):
        m_prev = m_ref[...]
        if bias_l.ndim == 0:
            m_new = jnp.maximum(m_prev, r_scr[...] * LOGIT_SCALE_LOG2 + bias_l)
            shift = -m_new if bias_block is not None else bias_l - m_new
        else:
            t = s_scr[...] * LOGIT_SCALE_LOG2 + bias_l
            m_new = jnp.maximum(m_prev, jnp.max(t, axis=-1, keepdims=True))
            s_scr[...] = t - m_new
            shift = None
        m_ref[...] = m_new
        rows.append((jnp.exp2(m_prev - m_new), shift))
    sums = [None, None]
    probs = [[], []]
    for c in range(0, tk, ATTN_KEYS_CHUNK):
        bias_c = None if bias_block is None else bias_block(c)
        for i, ((_, shift), s_scr) in enumerate(zip(rows, s_scrs)):
            s = s_scr[:, c:c + ATTN_KEYS_CHUNK]
            if shift is None:
                p = jnp.exp2(s)
            elif bias_c is None:
                p = jnp.exp2(s * LOGIT_SCALE_LOG2 + shift)
            else:
                p = jnp.exp2(s * LOGIT_SCALE_LOG2 + bias_c + shift)
            psum = jnp.sum(p, axis=-1, keepdims=True)
            sums[i] = psum if sums[i] is None else sums[i] + psum
            probs[i].append(p.astype(v_ref.dtype))
    for (alpha, _), psum, blocks, (_, _, _, l_ref, acc_ref) in zip(rows, sums, probs, maps):
        l_ref[...] = alpha * l_ref[...] + psum
        acc_ref[...] = alpha * acc_ref[...] + jnp.dot(
            jnp.concatenate(blocks, axis=1), v_ref[...], preferred_element_type=F32)


def _bucket_kernel(qmin_ref, qmax_ref, kmin_ref, kmax_ref, posq_ref, posk_ref, o_ref):
    qi = pl.program_id(0)
    ki = pl.program_id(1)
    const_bias = _tile_bias_is_const(qmin_ref, qmax_ref, kmin_ref, kmax_ref, qi, ki)[0]

    @pl.when(const_bias)
    def _():
        o_ref[...] = jnp.zeros_like(o_ref)

    @pl.when(jnp.logical_not(const_bias))
    def _():
        o_ref[...] = _rel_bucket(posk_ref[...] - posq_ref[...])


def _tile_bias_is_const(qmin_ref, qmax_ref, kmin_ref, kmax_ref, qi, ki):
    all_pos = kmin_ref[ki] - qmax_ref[qi] >= REL_MAX_DIST
    all_neg = kmax_ref[ki] - qmin_ref[qi] <= -REL_MAX_DIST
    return jnp.logical_or(all_pos, all_neg), all_pos


def _attn_kernel(qmin_ref, qmax_ref, kmin_ref, kmax_ref, bkt_blk_ref,
                 tab_smem, tab_range, tab_rows, bkt_ref,
                 lq1_ref, lk1_ref, lq2_ref, lk2_ref, g_ref,
                 q1_ref, q2_ref, k1_ref, k2_ref, v_ref,
                 o_ref,
                 m1_ref, l1_ref, acc1_ref, m2_ref, l2_ref, acc2_ref,
                 s1a_scr, s2a_scr, r1a_scr, r2a_scr,
                 s1b_scr, s2b_scr, r1b_scr, r2b_scr, *, nk):
    h = pl.program_id(0)
    qi = pl.program_id(1)
    j = pl.program_id(2)
    maps = [(q1_ref, k1_ref, m1_ref, l1_ref, acc1_ref),
            (q2_ref, k2_ref, m2_ref, l2_ref, acc2_ref)]
    slots = [((s1a_scr, s2a_scr), (r1a_scr, r2a_scr)),
             ((s1b_scr, s2b_scr), (r1b_scr, r2b_scr))]

    const_bias, all_pos = _tile_bias_is_const(qmin_ref, qmax_ref, kmin_ref, kmax_ref,
                                              qi, jnp.maximum(j - 1, 0))
    nb = REL_BUCKETS // 2

    narrow = (tab_range[0, h] - tab_range[1, h]) * LOG2E <= ATTN_BIAS_SPREAD_MAX
    per_elem = jnp.logical_not(const_bias)

    def lookup(c0, width):
        tq = bkt_ref.shape[0]
        row = jnp.broadcast_to(tab_rows[pl.ds(h, 1), :], (tq, LANES)) * LOG2E
        return jnp.concatenate(
            [jnp.take_along_axis(row, bkt_ref[:, c:c + LANES], axis=1)
             for c in range(c0, c0 + width, LANES)], axis=1)

    def bias_const():
        bias_l = LOG2E * jnp.where(all_pos, tab_smem[REL_BUCKETS - 1, h], tab_smem[nb - 1, h])
        return bias_l, None

    def bias_bounded():
        return LOG2E * tab_range[0, h], lambda c: lookup(c, ATTN_KEYS_CHUNK)

    def bias_exact():
        return lookup(0, bkt_ref.shape[1]), None

    def finalize():
        lam = (jnp.exp(jnp.sum(lq1_ref[...] * lk1_ref[...], axis=-1, keepdims=True))
               - jnp.exp(jnp.sum(lq2_ref[...] * lk2_ref[...], axis=-1, keepdims=True))
               + LAMBDA_INIT)
        o = acc1_ref[...] / l1_ref[...] - lam * (acc2_ref[...] / l2_ref[...])
        o = _rms(o, g_ref[...]) * (1.0 - LAMBDA_INIT)
        o_ref[...] = o.astype(o_ref.dtype)

    @pl.when(j == 0)
    def _():
        m1_ref[...] = jnp.full_like(m1_ref, -jnp.inf)
        m2_ref[...] = jnp.full_like(m2_ref, -jnp.inf)
        l1_ref[...] = jnp.zeros_like(l1_ref)
        l2_ref[...] = jnp.zeros_like(l2_ref)
        acc1_ref[...] = jnp.zeros_like(acc1_ref)
        acc2_ref[...] = jnp.zeros_like(acc2_ref)
        _attn_logits(maps, *slots[0])

    bias_kinds = ((const_bias, bias_const),
                  (jnp.logical_and(per_elem, narrow), bias_bounded),
                  (jnp.logical_and(per_elem, jnp.logical_not(narrow)), bias_exact))
    middle = jnp.logical_and(j > 0, j < nk)
    for parity in (0, 1):
        on_parity = jnp.logical_and(middle, j % 2 == parity)
        for is_kind, get_bias in bias_kinds:
            @pl.when(jnp.logical_and(on_parity, is_kind))
            def _(parity=parity, get_bias=get_bias):
                _attn_logits(maps, *slots[parity])
                _attn_accumulate(maps, *slots[1 - parity], v_ref, *get_bias())

    for is_kind, get_bias in bias_kinds:
        @pl.when(jnp.logical_and(j == nk, is_kind))
        def _(get_bias=get_bias):
            _attn_accumulate(maps, *slots[(nk - 1) % 2], v_ref, *get_bias())
            finalize()


def _diff_attention(qkv, positions, rel_table, lq1, lk1, lq2, lk2, subln_g, tq, tk):
    S = qkv.shape[0]
    H = N_HEADS
    nq, nk = S // tq, S // tk
    pos = positions.reshape(S).astype(jnp.int32)
    qmin = pos.reshape(nq, tq).min(axis=1)
    qmax = pos.reshape(nq, tq).max(axis=1)
    kmin = pos.reshape(nk, tk).min(axis=1)
    kmax = pos.reshape(nk, tk).max(axis=1)
    tab_rows = jnp.zeros((H, LANES), F32).at[:, :REL_BUCKETS].set(rel_table.T)

    buckets = pl.pallas_call(
        _bucket_kernel,
        grid_spec=pltpu.PrefetchScalarGridSpec(
            num_scalar_prefetch=4,
            grid=(nq, nk),
            in_specs=[pl.BlockSpec((tq, 1), lambda qi, ki, *_: (qi, 0)),
                      pl.BlockSpec((1, tk), lambda qi, ki, *_: (0, ki))],
            out_specs=pl.BlockSpec((tq, tk), lambda qi, ki, *_: (qi, ki)),
        ),
        out_shape=jax.ShapeDtypeStruct((S, S), jnp.int32),
        compiler_params=_cparams(("parallel", "arbitrary")),
        name="rel_buckets",
    )(qmin, qmax, kmin, kmax, pos.reshape(S, 1), pos.reshape(1, S))

    per_elem = jnp.logical_not(jnp.logical_or(
        kmin[None, :] - qmax[:, None] >= REL_MAX_DIST,
        kmax[None, :] - qmin[:, None] <= -REL_MAX_DIST))
    tile_of_step = jnp.maximum(jnp.arange(nk + 1, dtype=jnp.int32) - 1, 0)
    wanted = jnp.where(per_elem[:, tile_of_step], tile_of_step[None, :], -1)
    held = lax.cummax(wanted, axis=1)
    first = jnp.argmax(per_elem, axis=1).astype(jnp.int32)
    bkt_blk = jnp.where(held >= 0, held, first[:, None]).reshape(-1)

    kb = (H * 2 * HEAD_DIM) // HEAD_DIM
    vb = (2 * H * 2 * HEAD_DIM) // V_DIM
    vec = lambda n: pl.BlockSpec((1, n), lambda h, qi, j, *_: (0, 0))
    cur = lambda j: jnp.minimum(j, nk - 1)
    prev = lambda j: jnp.maximum(j - 1, 0)
    in_specs = [
        pl.BlockSpec(memory_space=pltpu.SMEM),
        pl.BlockSpec(memory_space=pltpu.SMEM),
        pl.BlockSpec((H, LANES), lambda h, qi, j, *_: (0, 0)),
        pl.BlockSpec((tq, tk),
                     lambda h, qi, j, a, b, c, d, blk: (qi, blk[qi * (nk + 1) + j])),
        vec(HEAD_DIM), vec(HEAD_DIM), vec(HEAD_DIM), vec(HEAD_DIM), vec(V_DIM),
        pl.BlockSpec((tq, HEAD_DIM), lambda h, qi, j, *_: (qi, 2 * h)),
        pl.BlockSpec((tq, HEAD_DIM), lambda h, qi, j, *_: (qi, 2 * h + 1)),
        pl.BlockSpec((tk, HEAD_DIM), lambda h, qi, j, *_: (cur(j), kb + 2 * h)),
        pl.BlockSpec((tk, HEAD_DIM), lambda h, qi, j, *_: (cur(j), kb + 2 * h + 1)),
        pl.BlockSpec((tk, V_DIM), lambda h, qi, j, *_: (prev(j), vb + h)),
    ]
    state = [pltpu.VMEM((tq, 1), F32), pltpu.VMEM((tq, 1), F32), pltpu.VMEM((tq, V_DIM), F32)]
    slot = [pltpu.VMEM((tq, tk), F32), pltpu.VMEM((tq, tk), F32),
            pltpu.VMEM((tq, 1), F32), pltpu.VMEM((tq, 1), F32)]
    grid_spec = pltpu.PrefetchScalarGridSpec(
        num_scalar_prefetch=5,
        grid=(H, nq, nk + 1),
        in_specs=in_specs,
        out_specs=pl.BlockSpec((tq, V_DIM), lambda h, qi, j, *_: (qi, h)),
        scratch_shapes=state + state + slot + slot,
    )
    return pl.pallas_call(
        functools.partial(_attn_kernel, nk=nk),
        grid_spec=grid_spec,
        out_shape=jax.ShapeDtypeStruct((S, H * V_DIM), BF16),
        compiler_params=_cparams(("parallel", "parallel", "arbitrary")),
        name="diff_attn",
    )(qmin, qmax, kmin, kmax, bkt_blk,
      rel_table, jnp.stack([rel_table.max(axis=0), rel_table.min(axis=0)]), tab_rows, buckets,
      lq1.reshape(1, -1), lk1.reshape(1, -1), lq2.reshape(1, -1), lk2.reshape(1, -1),
      subln_g.reshape(1, -1),
      qkv, qkv, qkv, qkv, qkv)


HALO = 2 * SUBLANES
CONV_COLS_CHUNK = 2 * LANES
CONV_ROWS_CHUNK = 64


def _conv_kernel(prev_ref, cur_ref, next_ref, w_ref, b_ref, g_ref, beta_ref,
                 o_ref, buf_ref, shift_ref, conv_ref):
    i = pl.program_id(0)
    ts, C = cur_ref.shape
    buf_ref[0:HALO, :] = jnp.where(i > 0, prev_ref[...], 0.0)
    buf_ref[HALO:HALO + ts, :] = cur_ref[...]
    buf_ref[HALO + ts:, :] = jnp.where(i < pl.num_programs(0) - 1, next_ref[...], 0.0)
    n_shift = shift_ref.shape[1]

    def col_body(cc, carry):
        c0 = pl.multiple_of(cc * CONV_COLS_CHUNK, CONV_COLS_CHUNK)
        cols = pl.ds(c0, CONV_COLS_CHUNK)
        for r in range(1, SUBLANES):
            shift_ref[r - 1] = buf_ref[r:r + n_shift, cols]
        for r0 in range(0, ts, CONV_ROWS_CHUNK):
            acc = jnp.broadcast_to(b_ref[:, cols], (CONV_ROWS_CHUNK, CONV_COLS_CHUNK))
            for j in range(CONV_WIDTH):
                start = HALO - CONV_PAD + j
                r, base = start % SUBLANES, start - start % SUBLANES + r0
                if r == 0:
                    taps = buf_ref[base:base + CONV_ROWS_CHUNK, cols]
                else:
                    taps = shift_ref[r - 1, base:base + CONV_ROWS_CHUNK, :]
                acc = acc + w_ref[j:j + 1, cols] * taps
            conv_ref[r0:r0 + CONV_ROWS_CHUNK, cols] = acc
        return carry

    lax.fori_loop(0, C // CONV_COLS_CHUNK, col_body, 0)

    c = conv_ref[...]
    mu = jnp.mean(c, axis=-1, keepdims=True)
    d = c - mu
    var = jnp.mean(d * d, axis=-1, keepdims=True)
    y = d * lax.rsqrt(var + EPS) * g_ref[...] + beta_ref[...]
    o_ref[...] = (y * _sigmoid(y)).astype(o_ref.dtype)


def _conv_module(c, w_dw, b_dw, ln_g, ln_b, ts=256):
    S, C = c.shape
    hb = ts // HALO
    last = S // HALO - 1
    vec = pl.BlockSpec((1, C), lambda i: (0, 0))
    return pl.pallas_call(
        _conv_kernel,
        grid=(S // ts,),
        in_specs=[
            pl.BlockSpec((HALO, C), lambda i: (jnp.maximum(i * hb - 1, 0), 0)),
            pl.BlockSpec((ts, C), lambda i: (i, 0)),
            pl.BlockSpec((HALO, C), lambda i: (jnp.minimum((i + 1) * hb, last), 0)),
            pl.BlockSpec((CONV_WIDTH, C), lambda i: (0, 0)),
            vec, vec, vec,
        ],
        out_specs=pl.BlockSpec((ts, C), lambda i: (i, 0)),
        out_shape=jax.ShapeDtypeStruct((S, C), BF16),
        scratch_shapes=[pltpu.VMEM((ts + 2 * HALO, C), F32),
                        pltpu.VMEM((SUBLANES - 1, ts + 2 * HALO - SUBLANES, CONV_COLS_CHUNK), F32),
                        pltpu.VMEM((ts, C), F32)],
        compiler_params=_cparams(("parallel",)),
        name="conv_module",
    )(c, c, c, w_dw.reshape(CONV_WIDTH, C), b_dw.reshape(1, C),
      ln_g.reshape(1, C), ln_b.reshape(1, C))


def kernel(x, p, positions, rel_table, mix_pre_g, w_in, lambda_q1, lambda_k1, lambda_q2, lambda_k2, subln_g, w_attn_o, w_dw, b_dw, conv_ln_g, conv_ln_b, w_conv_o, w_out, mix_post_g, ffn_pre_g, w_up, w_down, ffn_post_g, w_ple_gate, w_ple_proj, ple_post_g):
    B, S, D = x.shape
    assert B == 1 and w_in.shape[0] == 1
    x2d = x.reshape(S, D)
    qkv_cols = 3 * N_HEADS * 2 * HEAD_DIM
    conv_off = qkv_cols
    gate_off = qkv_cols + 2 * D

    w_attn_o_b = w_attn_o[0].astype(BF16)
    w_conv_o_b = w_conv_o[0].astype(BF16)
    p_b = p[0, 0].astype(BF16)

    tall = dict(tm=2048, tn=256, single_buffer_a=True)

    h = _prenorm(x2d, mix_pre_g[0])
    qkv = _fused_mm([(h, w_in[0], 0)], [], lambda a: a, qkv_cols, BF16, name="qkv_proj", **tall)
    glu = _fused_mm([(h, w_in[0], conv_off), (h, w_in[0], conv_off + D)], [],
                    lambda a, b: a * _sigmoid(b), D, F32, name="conv_glu", **tall)
    gates = _fused_mm([(h, w_in[0], gate_off)], [], _sigmoid, 2 * D, F32,
                      name="merge_gates", **tall)

    o = _diff_attention(qkv, positions, rel_table, lambda_q1[0], lambda_k1[0],
                        lambda_q2[0], lambda_k2[0], subln_g[0], tq=512, tk=1024)
    c = _conv_module(glu, w_dw[0], b_dw[0], conv_ln_g[0], conv_ln_b[0])

    merged = _fused_mm([(o, w_attn_o_b, 0), (c, w_conv_o_b, 0)],
                       [(gates, 0), (gates, D)],
                       lambda ya, yc, ga, gc: ga * ya + gc * yc,
                       D, BF16, tm=1024, tn=512, name="branch_merge", single_buffer_a=True)
    mix = _fused_mm([(merged, w_out[0], 0)], [], lambda a: a, D, F32, name="mix_out", **tall)
    x1, h2 = _postnorm_res(x2d, mix, mix_post_g[0], ffn_pre_g[0])

    u = _fused_mm([(h2, w_up[0], 0)], [], lambda a: jnp.square(jnp.maximum(a, 0.0)),
                  w_up.shape[2], BF16, name="ffn_up", **tall)
    dn = _mm_ktiled(u, w_down[0], tm=2048, tn=1024, tk=1024, name="ffn_down")
    x2, x2_b = _postnorm_res(x1, dn, ffn_post_g[0])

    ple = _fused_mm([(x2_b, w_ple_gate[0], 0), (p_b, w_ple_proj[0], 0)], [],
                    lambda a, e: _sigmoid(a) * e, D, F32, name="ple_gate", **tall)
    out = _final_res(x2, ple, ple_post_g[0])
    return out.reshape(B, S, D)
```

```python
import functools
import math

import jax
import jax.numpy as jnp
from jax import lax
from jax.experimental import pallas as pl
from jax.experimental.pallas import tpu as pltpu

N_HEADS = 16
HEAD_DIM = 128
V_DIM = 2 * HEAD_DIM
CONV_WIDTH = 31
CONV_PAD = CONV_WIDTH // 2
REL_BUCKETS = 32
REL_MAX_DIST = 128
EPS = 1e-6
LAMBDA_INIT = 0.8 - 0.6 * math.exp(-0.3 * 0)

LANES = 128
SUBLANES = 8
VMEM_LIMIT_BYTES = 56 * 1024 * 1024

F32 = jnp.float32
BF16 = jnp.bfloat16


def _cparams(sem, flags=None):
    return pltpu.CompilerParams(dimension_semantics=sem,
                                vmem_limit_bytes=VMEM_LIMIT_BYTES, flags=flags)


def _sigmoid(x):
    return 1.0 / (1.0 + jnp.exp(-x))


def _rms(x, g):
    return x * lax.rsqrt(jnp.mean(x * x, axis=-1, keepdims=True) + EPS) * g


def _prenorm_kernel(x_ref, g_ref, h_ref):
    h_ref[...] = _rms(x_ref[...], g_ref[...]).astype(h_ref.dtype)


def _prenorm(x, g, tr=256):
    S, D = x.shape
    return pl.pallas_call(
        _prenorm_kernel,
        grid=(S // tr,),
        in_specs=[pl.BlockSpec((tr, D), lambda i: (i, 0)),
                  pl.BlockSpec((1, D), lambda i: (0, 0))],
        out_specs=pl.BlockSpec((tr, D), lambda i: (i, 0)),
        out_shape=jax.ShapeDtypeStruct((S, D), BF16),
        compiler_params=_cparams(("parallel",)),
        name="prenorm",
    )(x, g.reshape(1, D))


def _postnorm_res_kernel(x_ref, y_ref, gpost_ref, *rest, with_prenorm):
    x_new = x_ref[...] + _rms(y_ref[...], gpost_ref[...])
    if with_prenorm:
        gpre_ref, xo_ref, h_ref = rest
        xo_ref[...] = x_new
        h_ref[...] = _rms(x_new, gpre_ref[...]).astype(h_ref.dtype)
    else:
        xo_ref, h_ref = rest
        xo_ref[...] = x_new
        h_ref[...] = x_new.astype(h_ref.dtype)


def _postnorm_res(x, y, g_post, g_pre=None, tr=256):
    S, D = x.shape
    row = pl.BlockSpec((tr, D), lambda i: (i, 0))
    vec = pl.BlockSpec((1, D), lambda i: (0, 0))
    args = [x, y, g_post.reshape(1, D)]
    in_specs = [row, row, vec]
    if g_pre is not None:
        args.append(g_pre.reshape(1, D))
        in_specs.append(vec)
    return pl.pallas_call(
        functools.partial(_postnorm_res_kernel, with_prenorm=g_pre is not None),
        grid=(S // tr,),
        in_specs=in_specs,
        out_specs=[row, row],
        out_shape=[jax.ShapeDtypeStruct((S, D), F32),
                   jax.ShapeDtypeStruct((S, D), BF16)],
        compiler_params=_cparams(("parallel",)),
        name="postnorm_res",
    )(*args)


def _final_kernel(x_ref, y_ref, g_ref, o_ref):
    o_ref[...] = x_ref[...] + _rms(y_ref[...], g_ref[...])


def _final_res(x, y, g, tr=256):
    S, D = x.shape
    row = pl.BlockSpec((tr, D), lambda i: (i, 0))
    return pl.pallas_call(
        _final_kernel,
        grid=(S // tr,),
        in_specs=[row, row, pl.BlockSpec((1, D), lambda i: (0, 0))],
        out_specs=row,
        out_shape=jax.ShapeDtypeStruct((S, D), F32),
        compiler_params=_cparams(("parallel",)),
        name="final_res",
    )(x, y, g.reshape(1, D))


def _fused_mm_kernel(*refs, a_index, epilogue):
    n_a = max(a_index) + 1
    n_pairs = len(a_index)
    out_ref = refs[-1]
    accs = [jnp.dot(refs[a_index[i]][...], refs[n_a + i][...].astype(BF16),
                    preferred_element_type=F32) for i in range(n_pairs)]
    extras = [r[...] for r in refs[n_a + n_pairs:-1]]
    out_ref[...] = epilogue(*accs, *extras).astype(out_ref.dtype)


def _fused_mm(pairs, extras, epilogue, n_out, out_dtype, tm, tn, name, single_buffer_a=False):
    M = pairs[0][0].shape[0]
    a_list, a_index = [], []
    for a, _, _ in pairs:
        for idx, seen in enumerate(a_list):
            if seen is a:
                a_index.append(idx)
                break
        else:
            a_index.append(len(a_list))
            a_list.append(a)
    a_mode = dict(pipeline_mode=pl.Buffered(1)) if single_buffer_a else {}
    in_specs = [pl.BlockSpec((tm, a.shape[1]), lambda m, n: (m, 0), **a_mode) for a in a_list]
    args = list(a_list)
    for a, w, off in pairs:
        ob = off // tn
        in_specs.append(pl.BlockSpec((a.shape[1], tn), lambda m, n, ob=ob: (0, n + ob)))
        args.append(w)
    for e, off in extras:
        ob = off // tn
        in_specs.append(pl.BlockSpec((tm, tn), lambda m, n, ob=ob: (m, n + ob)))
        args.append(e)
    return pl.pallas_call(
        functools.partial(_fused_mm_kernel, a_index=tuple(a_index), epilogue=epilogue),
        grid=(M // tm, n_out // tn),
        in_specs=in_specs,
        out_specs=pl.BlockSpec((tm, tn), lambda m, n: (m, n)),
        out_shape=jax.ShapeDtypeStruct((M, n_out), out_dtype),
        compiler_params=_cparams(("parallel", "arbitrary")),
        name=name,
    )(*args)


def _mm_ktiled_kernel(a_ref, w_ref, o_ref):
    k = pl.program_id(2)

    def part():
        return jnp.dot(a_ref[...], w_ref[...].astype(BF16), preferred_element_type=F32)

    @pl.when(k == 0)
    def _():
        o_ref[...] = part()

    @pl.when(k > 0)
    def _():
        o_ref[...] += part()


def _mm_ktiled(a, w, tm, tn, tk, name):
    M, K = a.shape
    N = w.shape[1]
    return pl.pallas_call(
        _mm_ktiled_kernel,
        grid=(M // tm, N // tn, K // tk),
        in_specs=[pl.BlockSpec((tm, tk), lambda m, n, k: (m, k)),
                  pl.BlockSpec((tk, tn), lambda m, n, k: (k, n))],
        out_specs=pl.BlockSpec((tm, tn), lambda m, n, k: (m, n)),
        out_shape=jax.ShapeDtypeStruct((M, N), F32),
        compiler_params=_cparams(("parallel", "parallel", "arbitrary")),
        name=name,
    )(a, w)


def _rel_bucket(rel):
    nb = REL_BUCKETS // 2
    max_exact = nb // 2
    ret = jnp.where(rel > 0, nb, 0)
    n = jnp.abs(rel)
    nf = jnp.maximum(n, 1).astype(F32)
    large = max_exact + (jnp.log(nf / max_exact) / math.log(REL_MAX_DIST / max_exact)
                         * (nb - max_exact)).astype(jnp.int32)
    large = jnp.minimum(large, nb - 1)
    return ret + jnp.where(n < max_exact, n, large)


LOG2E = math.log2(math.e)
LOGIT_SCALE_LOG2 = (HEAD_DIM ** -0.5) * LOG2E
ATTN_KEYS_CHUNK = 256


def _attn_logits(maps, s_scrs, r_scrs):
    logits = [lax.dot_general(q_ref[...], k_ref[...], (((1,), (1,)), ((), ())),
                              preferred_element_type=F32)
              for q_ref, k_ref, _, _, _ in maps]
    for s, s_scr, r_scr in zip(logits, s_scrs, r_scrs):
        s_scr[...] = s
        r_scr[...] = jnp.max(s, axis=-1, keepdims=True)


def _attn_accumulate(maps, s_scrs, r_scrs, v_ref, bias_l):
    tk = v_ref.shape[0]
    rows = []
    for (_, _, m_ref, _, _), s_scr, r_scr in zip(maps, s_scrs, r_scrs):
        m_prev = m_ref[...]
        if bias_l.ndim == 0:
            m_new = jnp.maximum(m_prev, r_scr[...] * LOGIT_SCALE_LOG2 + bias_l)
            shift = bias_l - m_new
        else:
            t = s_scr[...] * LOGIT_SCALE_LOG2 + bias_l
            m_new = jnp.maximum(m_prev, jnp.max(t, axis=-1, keepdims=True))
            s_scr[...] = t - m_new
            shift = None
        m_ref[...] = m_new
        rows.append((jnp.exp2(m_prev - m_new), shift))
    sums = [None, None]
    probs = [[], []]
    for c in range(0, tk, ATTN_KEYS_CHUNK):
        for i, ((_, shift), s_scr) in enumerate(zip(rows, s_scrs)):
            s = s_scr[:, c:c + ATTN_KEYS_CHUNK]
            p = jnp.exp2(s if shift is None else s * LOGIT_SCALE_LOG2 + shift)
            psum = jnp.sum(p, axis=-1, keepdims=True)
            sums[i] = psum if sums[i] is None else sums[i] + psum
            probs[i].append(p.astype(v_ref.dtype))
    for (alpha, _), psum, blocks, (_, _, _, l_ref, acc_ref) in zip(rows, sums, probs, maps):
        l_ref[...] = alpha * l_ref[...] + psum
        acc_ref[...] = alpha * acc_ref[...] + jnp.dot(
            jnp.concatenate(blocks, axis=1), v_ref[...], preferred_element_type=F32)


def _bucket_kernel(qmin_ref, qmax_ref, kmin_ref, kmax_ref, posq_ref, posk_ref, o_ref):
    qi = pl.program_id(0)
    ki = pl.program_id(1)
    const_bias = _tile_bias_is_const(qmin_ref, qmax_ref, kmin_ref, kmax_ref, qi, ki)[0]

    @pl.when(const_bias)
    def _():
        o_ref[...] = jnp.zeros_like(o_ref)

    @pl.when(jnp.logical_not(const_bias))
    def _():
        o_ref[...] = _rel_bucket(posk_ref[...] - posq_ref[...]).astype(o_ref.dtype)


def _tile_bias_is_const(qmin_ref, qmax_ref, kmin_ref, kmax_ref, qi, ki):
    all_pos = kmin_ref[ki] - qmax_ref[qi] >= REL_MAX_DIST
    all_neg = kmax_ref[ki] - qmin_ref[qi] <= -REL_MAX_DIST
    return jnp.logical_or(all_pos, all_neg), all_pos


def _attn_kernel(qmin_ref, qmax_ref, kmin_ref, kmax_ref, bkt_blk_ref,
                 tab_smem, tab_rows, bkt_ref,
                 lq1_ref, lk1_ref, lq2_ref, lk2_ref, g_ref,
                 q_ref, k_ref, v_ref,
                 o_ref,
                 m1_ref, l1_ref, acc1_ref, m2_ref, l2_ref, acc2_ref,
                 s1a_scr, s2a_scr, r1a_scr, r2a_scr,
                 s1b_scr, s2b_scr, r1b_scr, r2b_scr, *, nk):
    h = pl.program_id(0)
    qi = pl.program_id(1)
    j = pl.program_id(2)
    lo, hi = pl.ds(0, HEAD_DIM), pl.ds(HEAD_DIM, HEAD_DIM)
    maps = [(q_ref.at[:, lo], k_ref.at[:, lo], m1_ref, l1_ref, acc1_ref),
            (q_ref.at[:, hi], k_ref.at[:, hi], m2_ref, l2_ref, acc2_ref)]
    slots = [((s1a_scr, s2a_scr), (r1a_scr, r2a_scr)),
             ((s1b_scr, s2b_scr), (r1b_scr, r2b_scr))]

    const_bias, all_pos = _tile_bias_is_const(qmin_ref, qmax_ref, kmin_ref, kmax_ref,
                                              qi, jnp.maximum(j - 1, 0))
    nb = REL_BUCKETS // 2

    def bias_scalar():
        return LOG2E * jnp.where(all_pos, tab_smem[REL_BUCKETS - 1, h], tab_smem[nb - 1, h])

    def bias_tile():
        tq, tk = bkt_ref.shape
        row = jnp.broadcast_to(tab_rows[pl.ds(h, 1), :], (tq, LANES)) * LOG2E
        return jnp.concatenate(
            [jnp.take_along_axis(row, bkt_ref[:, c:c + LANES].astype(jnp.int32), axis=1)
             for c in range(0, tk, LANES)], axis=1)

    def finalize():
        lam = (jnp.exp(jnp.sum(lq1_ref[...] * lk1_ref[...], axis=-1, keepdims=True))
               - jnp.exp(jnp.sum(lq2_ref[...] * lk2_ref[...], axis=-1, keepdims=True))
               + LAMBDA_INIT)
        o = acc1_ref[...] / l1_ref[...] - lam * (acc2_ref[...] / l2_ref[...])
        o = _rms(o, g_ref[...]) * (1.0 - LAMBDA_INIT)
        o_ref[...] = o.astype(o_ref.dtype)

    @pl.when(j == 0)
    def _():
        m1_ref[...] = jnp.full_like(m1_ref, -jnp.inf)
        m2_ref[...] = jnp.full_like(m2_ref, -jnp.inf)
        l1_ref[...] = jnp.zeros_like(l1_ref)
        l2_ref[...] = jnp.zeros_like(l2_ref)
        acc1_ref[...] = jnp.zeros_like(acc1_ref)
        acc2_ref[...] = jnp.zeros_like(acc2_ref)
        _attn_logits(maps, *slots[0])

    bias_kinds = ((const_bias, bias_scalar), (jnp.logical_not(const_bias), bias_tile))
    middle = jnp.logical_and(j > 0, j < nk)
    for parity in (0, 1):
        on_parity = jnp.logical_and(middle, j % 2 == parity)
        for is_kind, get_bias in bias_kinds:
            @pl.when(jnp.logical_and(on_parity, is_kind))
            def _(parity=parity, get_bias=get_bias):
                bias_l = get_bias()
                _attn_logits(maps, *slots[parity])
                _attn_accumulate(maps, *slots[1 - parity], v_ref, bias_l)

    for is_kind, get_bias in bias_kinds:
        @pl.when(jnp.logical_and(j == nk, is_kind))
        def _(get_bias=get_bias):
            _attn_accumulate(maps, *slots[(nk - 1) % 2], v_ref, get_bias())
            finalize()


def _diff_attention(qkv, positions, rel_table, lq1, lk1, lq2, lk2, subln_g, tq, tk):
    S = qkv.shape[0]
    H = N_HEADS
    nq, nk = S // tq, S // tk
    pos = positions.reshape(S).astype(jnp.int32)
    qmin = pos.reshape(nq, tq).min(axis=1)
    qmax = pos.reshape(nq, tq).max(axis=1)
    kmin = pos.reshape(nk, tk).min(axis=1)
    kmax = pos.reshape(nk, tk).max(axis=1)
    tab_rows = jnp.zeros((H, LANES), F32).at[:, :REL_BUCKETS].set(rel_table.T)

    buckets = pl.pallas_call(
        _bucket_kernel,
        grid_spec=pltpu.PrefetchScalarGridSpec(
            num_scalar_prefetch=4,
            grid=(nq, nk),
            in_specs=[pl.BlockSpec((tq, 1), lambda qi, ki, *_: (qi, 0)),
                      pl.BlockSpec((1, tk), lambda qi, ki, *_: (0, ki))],
            out_specs=pl.BlockSpec((tq, tk), lambda qi, ki, *_: (qi, ki)),
        ),
        out_shape=jax.ShapeDtypeStruct((S, S), jnp.int8),
        compiler_params=_cparams(("parallel", "arbitrary")),
        name="rel_buckets",
    )(qmin, qmax, kmin, kmax, pos.reshape(S, 1), pos.reshape(1, S))

    per_elem = jnp.logical_not(jnp.logical_or(
        kmin[None, :] - qmax[:, None] >= REL_MAX_DIST,
        kmax[None, :] - qmin[:, None] <= -REL_MAX_DIST))
    tile_of_step = jnp.maximum(jnp.arange(nk + 1, dtype=jnp.int32) - 1, 0)
    wanted = jnp.where(per_elem[:, tile_of_step], tile_of_step[None, :], -1)
    held = lax.cummax(wanted, axis=1)
    first = jnp.argmax(per_elem, axis=1).astype(jnp.int32)
    bkt_blk = jnp.where(held >= 0, held, first[:, None]).reshape(-1)

    kb = (H * 2 * HEAD_DIM) // (2 * HEAD_DIM)
    vb = (2 * H * 2 * HEAD_DIM) // V_DIM
    vec = lambda n: pl.BlockSpec((1, n), lambda h, qi, j, *_: (0, 0))
    cur = lambda j: jnp.minimum(j, nk - 1)
    prev = lambda j: jnp.maximum(j - 1, 0)
    in_specs = [
        pl.BlockSpec(memory_space=pltpu.SMEM),
        pl.BlockSpec((H, LANES), lambda h, qi, j, *_: (0, 0)),
        pl.BlockSpec((tq, tk),
                     lambda h, qi, j, a, b, c, d, blk: (qi, blk[qi * (nk + 1) + j])),
        vec(HEAD_DIM), vec(HEAD_DIM), vec(HEAD_DIM), vec(HEAD_DIM), vec(V_DIM),
        pl.BlockSpec((tq, 2 * HEAD_DIM), lambda h, qi, j, *_: (qi, h)),
        pl.BlockSpec((tk, 2 * HEAD_DIM), lambda h, qi, j, *_: (cur(j), kb + h)),
        pl.BlockSpec((tk, V_DIM), lambda h, qi, j, *_: (prev(j), vb + h)),
    ]
    state = [pltpu.VMEM((tq, 1), F32), pltpu.VMEM((tq, 1), F32), pltpu.VMEM((tq, V_DIM), F32)]
    slot = [pltpu.VMEM((tq, tk), F32), pltpu.VMEM((tq, tk), F32),
            pltpu.VMEM((tq, 1), F32), pltpu.VMEM((tq, 1), F32)]
    grid_spec = pltpu.PrefetchScalarGridSpec(
        num_scalar_prefetch=5,
        grid=(H, nq, nk + 1),
        in_specs=in_specs,
        out_specs=pl.BlockSpec((tq, V_DIM), lambda h, qi, j, *_: (qi, h)),
        scratch_shapes=state + state + slot + slot,
    )
    return pl.pallas_call(
        functools.partial(_attn_kernel, nk=nk),
        grid_spec=grid_spec,
        out_shape=jax.ShapeDtypeStruct((S, H * V_DIM), BF16),
        compiler_params=_cparams(("parallel", "parallel", "arbitrary")),
        name="diff_attn",
    )(qmin, qmax, kmin, kmax, bkt_blk,
      rel_table, tab_rows, buckets,
      lq1.reshape(1, -1), lk1.reshape(1, -1), lq2.reshape(1, -1), lk2.reshape(1, -1),
      subln_g.reshape(1, -1),
      qkv, qkv, qkv)


HALO = 2 * SUBLANES
CONV_COLS_CHUNK = 2 * LANES
CONV_ROWS_CHUNK = 64


def _conv_kernel(prev_ref, cur_ref, next_ref, w_ref, b_ref, g_ref, beta_ref,
                 o_ref, buf_ref, shift_ref, conv_ref):
    i = pl.program_id(0)
    ts, C = cur_ref.shape
    buf_ref[0:HALO, :] = jnp.where(i > 0, prev_ref[...], 0.0)
    buf_ref[HALO:HALO + ts, :] = cur_ref[...]
    buf_ref[HALO + ts:, :] = jnp.where(i < pl.num_programs(0) - 1, next_ref[...], 0.0)
    n_shift = shift_ref.shape[1]

    def col_body(cc, carry):
        c0 = pl.multiple_of(cc * CONV_COLS_CHUNK, CONV_COLS_CHUNK)
        cols = pl.ds(c0, CONV_COLS_CHUNK)
        for r in range(1, SUBLANES):
            shift_ref[r - 1] = buf_ref[r:r + n_shift, cols]
        for r0 in range(0, ts, CONV_ROWS_CHUNK):
            acc = jnp.broadcast_to(b_ref[:, cols], (CONV_ROWS_CHUNK, CONV_COLS_CHUNK))
            for j in range(CONV_WIDTH):
                start = HALO - CONV_PAD + j
                r, base = start % SUBLANES, start - start % SUBLANES + r0
                if r == 0:
                    taps = buf_ref[base:base + CONV_ROWS_CHUNK, cols]
                else:
                    taps = shift_ref[r - 1, base:base + CONV_ROWS_CHUNK, :]
                acc = acc + w_ref[j:j + 1, cols] * taps
            conv_ref[r0:r0 + CONV_ROWS_CHUNK, cols] = acc
        return carry

    lax.fori_loop(0, C // CONV_COLS_CHUNK, col_body, 0)

    c = conv_ref[...]
    mu = jnp.mean(c, axis=-1, keepdims=True)
    d = c - mu
    var = jnp.mean(d * d, axis=-1, keepdims=True)
    y = d * lax.rsqrt(var + EPS) * g_ref[...] + beta_ref[...]
    o_ref[...] = (y * _sigmoid(y)).astype(o_ref.dtype)


def _conv_module(c, w_dw, b_dw, ln_g, ln_b, ts=256):
    S, C = c.shape
    hb = ts // HALO
    last = S // HALO - 1
    vec = pl.BlockSpec((1, C), lambda i: (0, 0))
    return pl.pallas_call(
        _conv_kernel,
        grid=(S // ts,),
        in_specs=[
            pl.BlockSpec((HALO, C), lambda i: (jnp.maximum(i * hb - 1, 0), 0)),
            pl.BlockSpec((ts, C), lambda i: (i, 0)),
            pl.BlockSpec((HALO, C), lambda i: (jnp.minimum((i + 1) * hb, last), 0)),
            pl.BlockSpec((CONV_WIDTH, C), lambda i: (0, 0)),
            vec, vec, vec,
        ],
        out_specs=pl.BlockSpec((ts, C), lambda i: (i, 0)),
        out_shape=jax.ShapeDtypeStruct((S, C), BF16),
        scratch_shapes=[pltpu.VMEM((ts + 2 * HALO, C), F32),
                        pltpu.VMEM((SUBLANES - 1, ts + 2 * HALO - SUBLANES, CONV_COLS_CHUNK), F32),
                        pltpu.VMEM((ts, C), F32)],
        compiler_params=_cparams(("parallel",)),
        name="conv_module",
    )(c, c, c, w_dw.reshape(CONV_WIDTH, C), b_dw.reshape(1, C),
      ln_g.reshape(1, C), ln_b.reshape(1, C))


def kernel(x, p, positions, rel_table, mix_pre_g, w_in, lambda_q1, lambda_k1, lambda_q2, lambda_k2, subln_g, w_attn_o, w_dw, b_dw, conv_ln_g, conv_ln_b, w_conv_o, w_out, mix_post_g, ffn_pre_g, w_up, w_down, ffn_post_g, w_ple_gate, w_ple_proj, ple_post_g):
    B, S, D = x.shape
    assert B == 1 and w_in.shape[0] == 1
    x2d = x.reshape(S, D)
    qkv_cols = 3 * N_HEADS * 2 * HEAD_DIM
    conv_off = qkv_cols
    gate_off = qkv_cols + 2 * D

    w_attn_o_b = w_attn_o[0].astype(BF16)
    w_conv_o_b = w_conv_o[0].astype(BF16)
    p_b = p[0, 0].astype(BF16)

    tall = dict(tm=2048, tn=512, single_buffer_a=True)
    tall_two_weights = dict(tm=2048, tn=256, single_buffer_a=True)

    h = _prenorm(x2d, mix_pre_g[0])
    qkv = _fused_mm([(h, w_in[0], 0)], [], lambda a: a, qkv_cols, BF16, name="qkv_proj", **tall)
    glu = _fused_mm([(h, w_in[0], conv_off), (h, w_in[0], conv_off + D)], [],
                    lambda a, b: a * _sigmoid(b), D, F32, name="conv_glu", **tall_two_weights)
    gates = _fused_mm([(h, w_in[0], gate_off)], [], _sigmoid, 2 * D, F32,
                      name="merge_gates", **tall)

    o = _diff_attention(qkv, positions, rel_table, lambda_q1[0], lambda_k1[0],
                        lambda_q2[0], lambda_k2[0], subln_g[0], tq=512, tk=1024)
    c = _conv_module(glu, w_dw[0], b_dw[0], conv_ln_g[0], conv_ln_b[0])

    merged = _fused_mm([(o, w_attn_o_b, 0), (c, w_conv_o_b, 0)],
                       [(gates, 0), (gates, D)],
                       lambda ya, yc, ga, gc: ga * ya + gc * yc,
                       D, BF16, tm=1024, tn=512, name="branch_merge", single_buffer_a=True)
    mix = _fused_mm([(merged, w_out[0], 0)], [], lambda a: a, D, F32, name="mix_out", **tall)
    x1, h2 = _postnorm_res(x2d, mix, mix_post_g[0], ffn_pre_g[0])

    u = _fused_mm([(h2, w_up[0], 0)], [], lambda a: jnp.square(jnp.maximum(a, 0.0)),
                  w_up.shape[2], BF16, name="ffn_up", **tall)
    dn = _mm_ktiled(u, w_down[0], tm=2048, tn=1024, tk=1024, name="ffn_down")
    x2, x2_b = _postnorm_res(x1, dn, ffn_post_g[0])

    ple = _fused_mm([(x2_b, w_ple_gate[0], 0), (p_b, w_ple_proj[0], 0)], [],
                    lambda a, e: _sigmoid(a) * e, D, F32, name="ple_gate", **tall)
    out = _final_res(x2, ple, ple_post_g[0])
    return out.reshape(B, S, D)
```

```python
import functools
import math

import jax
import jax.numpy as jnp
from jax import lax
from jax.experimental import pallas as pl
from jax.experimental.pallas import tpu as pltpu

N_HEADS = 16
HEAD_DIM = 128
V_DIM = 2 * HEAD_DIM
CONV_WIDTH = 31
CONV_PAD = CONV_WIDTH // 2
REL_BUCKETS = 32
REL_MAX_DIST = 128
EPS = 1e-6
LAMBDA_INIT = 0.8 - 0.6 * math.exp(-0.3 * 0)

LANES = 128
SUBLANES = 8
VMEM_LIMIT_BYTES = 56 * 1024 * 1024

F32 = jnp.float32
BF16 = jnp.bfloat16


def _cparams(sem, flags=None):
    return pltpu.CompilerParams(dimension_semantics=sem,
                                vmem_limit_bytes=VMEM_LIMIT_BYTES, flags=flags)


def _sigmoid(x):
    return 1.0 / (1.0 + jnp.exp(-x))


def _rms(x, g):
    return x * lax.rsqrt(jnp.mean(x * x, axis=-1, keepdims=True) + EPS) * g


def _prenorm_kernel(x_ref, g_ref, h_ref):
    h_ref[...] = _rms(x_ref[...], g_ref[...]).astype(h_ref.dtype)


def _prenorm(x, g, tr=256):
    S, D = x.shape
    return pl.pallas_call(
        _prenorm_kernel,
        grid=(S // tr,),
        in_specs=[pl.BlockSpec((tr, D), lambda i: (i, 0)),
                  pl.BlockSpec((1, D), lambda i: (0, 0))],
        out_specs=pl.BlockSpec((tr, D), lambda i: (i, 0)),
        out_shape=jax.ShapeDtypeStruct((S, D), BF16),
        compiler_params=_cparams(("parallel",)),
        name="prenorm",
    )(x, g.reshape(1, D))


def _postnorm_res_kernel(x_ref, y_ref, gpost_ref, *rest, with_prenorm):
    x_new = x_ref[...] + _rms(y_ref[...], gpost_ref[...])
    if with_prenorm:
        gpre_ref, xo_ref, h_ref = rest
        xo_ref[...] = x_new
        h_ref[...] = _rms(x_new, gpre_ref[...]).astype(h_ref.dtype)
    else:
        xo_ref, h_ref = rest
        xo_ref[...] = x_new
        h_ref[...] = x_new.astype(h_ref.dtype)


def _postnorm_res(x, y, g_post, g_pre=None, tr=256):
    S, D = x.shape
    row = pl.BlockSpec((tr, D), lambda i: (i, 0))
    vec = pl.BlockSpec((1, D), lambda i: (0, 0))
    args = [x, y, g_post.reshape(1, D)]
    in_specs = [row, row, vec]
    if g_pre is not None:
        args.append(g_pre.reshape(1, D))
        in_specs.append(vec)
    return pl.pallas_call(
        functools.partial(_postnorm_res_kernel, with_prenorm=g_pre is not None),
        grid=(S // tr,),
        in_specs=in_specs,
        out_specs=[row, row],
        out_shape=[jax.ShapeDtypeStruct((S, D), F32),
                   jax.ShapeDtypeStruct((S, D), BF16)],
        compiler_params=_cparams(("parallel",)),
        name="postnorm_res",
    )(*args)


def _final_kernel(x_ref, y_ref, g_ref, o_ref):
    o_ref[...] = x_ref[...] + _rms(y_ref[...], g_ref[...])


def _final_res(x, y, g, tr=256):
    S, D = x.shape
    row = pl.BlockSpec((tr, D), lambda i: (i, 0))
    return pl.pallas_call(
        _final_kernel,
        grid=(S // tr,),
        in_specs=[row, row, pl.BlockSpec((1, D), lambda i: (0, 0))],
        out_specs=row,
        out_shape=jax.ShapeDtypeStruct((S, D), F32),
        compiler_params=_cparams(("parallel",)),
        name="final_res",
    )(x, y, g.reshape(1, D))


def _fused_mm_kernel(*refs, a_index, epilogue, transpose_out):
    n_a = max(a_index) + 1
    n_pairs = len(a_index)
    out_ref = refs[-1]
    accs = [jnp.dot(refs[a_index[i]][...], refs[n_a + i][...].astype(BF16),
                    preferred_element_type=F32) for i in range(n_pairs)]
    extras = [r[...] for r in refs[n_a + n_pairs:-1]]
    out = epilogue(*accs, *extras)
    out_ref[...] = (out.T if transpose_out else out).astype(out_ref.dtype)


def _fused_mm(pairs, extras, epilogue, n_out, out_dtype, tm, tn, name, single_buffer_a=False,
              transpose_out=False):
    M = pairs[0][0].shape[0]
    a_list, a_index = [], []
    for a, _, _ in pairs:
        for idx, seen in enumerate(a_list):
            if seen is a:
                a_index.append(idx)
                break
        else:
            a_index.append(len(a_list))
            a_list.append(a)
    a_mode = dict(pipeline_mode=pl.Buffered(1)) if single_buffer_a else {}
    in_specs = [pl.BlockSpec((tm, a.shape[1]), lambda m, n: (m, 0), **a_mode) for a in a_list]
    args = list(a_list)
    for a, w, off in pairs:
        ob = off // tn
        in_specs.append(pl.BlockSpec((a.shape[1], tn), lambda m, n, ob=ob: (0, n + ob)))
        args.append(w)
    for e, off in extras:
        ob = off // tn
        in_specs.append(pl.BlockSpec((tm, tn), lambda m, n, ob=ob: (m, n + ob)))
        args.append(e)
    return pl.pallas_call(
        functools.partial(_fused_mm_kernel, a_index=tuple(a_index), epilogue=epilogue,
                          transpose_out=transpose_out),
        grid=(M // tm, n_out // tn),
        in_specs=in_specs,
        out_specs=(pl.BlockSpec((tn, tm), lambda m, n: (n, m)) if transpose_out
                   else pl.BlockSpec((tm, tn), lambda m, n: (m, n))),
        out_shape=jax.ShapeDtypeStruct((n_out, M) if transpose_out else (M, n_out), out_dtype),
        compiler_params=_cparams(("parallel", "arbitrary")),
        name=name,
    )(*args)


def _mm_ktiled_kernel(a_ref, w_ref, o_ref):
    k = pl.program_id(2)

    def part():
        return jnp.dot(a_ref[...], w_ref[...].astype(BF16), preferred_element_type=F32)

    @pl.when(k == 0)
    def _():
        o_ref[...] = part()

    @pl.when(k > 0)
    def _():
        o_ref[...] += part()


def _mm_ktiled(a, w, tm, tn, tk, name):
    M, K = a.shape
    N = w.shape[1]
    return pl.pallas_call(
        _mm_ktiled_kernel,
        grid=(M // tm, N // tn, K // tk),
        in_specs=[pl.BlockSpec((tm, tk), lambda m, n, k: (m, k)),
                  pl.BlockSpec((tk, tn), lambda m, n, k: (k, n))],
        out_specs=pl.BlockSpec((tm, tn), lambda m, n, k: (m, n)),
        out_shape=jax.ShapeDtypeStruct((M, N), F32),
        compiler_params=_cparams(("parallel", "parallel", "arbitrary")),
        name=name,
    )(a, w)


def _rel_bucket(rel):
    nb = REL_BUCKETS // 2
    max_exact = nb // 2
    ret = jnp.where(rel > 0, nb, 0)
    n = jnp.abs(rel)
    nf = jnp.maximum(n, 1).astype(F32)
    large = max_exact + (jnp.log(nf / max_exact) / math.log(REL_MAX_DIST / max_exact)
                         * (nb - max_exact)).astype(jnp.int32)
    large = jnp.minimum(large, nb - 1)
    return ret + jnp.where(n < max_exact, n, large)


LOG2E = math.log2(math.e)
LOGIT_SCALE_LOG2 = (HEAD_DIM ** -0.5) * LOG2E
ATTN_KEYS_CHUNK = 256


def _attn_logits(maps, s_scrs, r_scrs):
    logits = [jnp.dot(k_ref[...], qt_ref[...], preferred_element_type=F32)
              for qt_ref, k_ref, _, _, _ in maps]
    for s, s_scr, r_scr in zip(logits, s_scrs, r_scrs):
        s_scr[...] = s
        r_scr[...] = jnp.max(s, axis=0, keepdims=True)


def _attn_accumulate(maps, s_scrs, r_scrs, vt_ref, bias_l):
    tk = vt_ref.shape[1]
    rows = []
    for (_, _, m_ref, _, _), s_scr, r_scr in zip(maps, s_scrs, r_scrs):
        m_prev = m_ref[...]
        if bias_l.ndim == 0:
            m_new = jnp.maximum(m_prev, r_scr[...] * LOGIT_SCALE_LOG2 + bias_l)
            shift = bias_l - m_new
        else:
            t = s_scr[...] * LOGIT_SCALE_LOG2 + bias_l
            m_new = jnp.maximum(m_prev, jnp.max(t, axis=0, keepdims=True))
            s_scr[...] = t - m_new
            shift = None
        m_ref[...] = m_new
        rows.append((jnp.exp2(m_prev - m_new), shift))
    sums = [None, None]
    probs = [[], []]
    for c in range(0, tk, ATTN_KEYS_CHUNK):
        for i, ((_, shift), s_scr) in enumerate(zip(rows, s_scrs)):
            s = s_scr[c:c + ATTN_KEYS_CHUNK, :]
            p = jnp.exp2(s if shift is None else s * LOGIT_SCALE_LOG2 + shift)
            psum = jnp.sum(p, axis=0, keepdims=True)
            sums[i] = psum if sums[i] is None else sums[i] + psum
            probs[i].append(p.astype(vt_ref.dtype))
    for (alpha, _), psum, blocks, (_, _, _, l_ref, acc_ref) in zip(rows, sums, probs, maps):
        l_ref[...] = alpha * l_ref[...] + psum
        acc_ref[...] = alpha * acc_ref[...] + jnp.dot(
            vt_ref[...], jnp.concatenate(blocks, axis=0), preferred_element_type=F32)


def _bucket_kernel(qmin_ref, qmax_ref, kmin_ref, kmax_ref, posq_ref, posk_ref, o_ref):
    ki = pl.program_id(0)
    qi = pl.program_id(1)
    const_bias = _tile_bias_is_const(qmin_ref, qmax_ref, kmin_ref, kmax_ref, qi, ki)[0]

    @pl.when(const_bias)
    def _():
        o_ref[...] = jnp.zeros_like(o_ref)

    @pl.when(jnp.logical_not(const_bias))
    def _():
        o_ref[...] = _rel_bucket(posk_ref[...] - posq_ref[...]).astype(o_ref.dtype)


def _tile_bias_is_const(qmin_ref, qmax_ref, kmin_ref, kmax_ref, qi, ki):
    all_pos = kmin_ref[ki] - qmax_ref[qi] >= REL_MAX_DIST
    all_neg = kmax_ref[ki] - qmin_ref[qi] <= -REL_MAX_DIST
    return jnp.logical_or(all_pos, all_neg), all_pos


def _attn_kernel(qmin_ref, qmax_ref, kmin_ref, kmax_ref, bkt_blk_ref,
                 tab_smem, tab_rows, bkt_ref,
                 lq1_ref, lk1_ref, lq2_ref, lk2_ref, g_ref,
                 qt_ref, k_ref, vt_ref,
                 o_ref,
                 m1_ref, l1_ref, acc1_ref, m2_ref, l2_ref, acc2_ref,
                 s1a_scr, s2a_scr, r1a_scr, r2a_scr,
                 s1b_scr, s2b_scr, r1b_scr, r2b_scr, *, nk):
    h = pl.program_id(0)
    qi = pl.program_id(1)
    j = pl.program_id(2)
    lo, hi = pl.ds(0, HEAD_DIM), pl.ds(HEAD_DIM, HEAD_DIM)
    maps = [(qt_ref.at[lo, :], k_ref.at[:, lo], m1_ref, l1_ref, acc1_ref),
            (qt_ref.at[hi, :], k_ref.at[:, hi], m2_ref, l2_ref, acc2_ref)]
    slots = [((s1a_scr, s2a_scr), (r1a_scr, r2a_scr)),
             ((s1b_scr, s2b_scr), (r1b_scr, r2b_scr))]

    const_bias, all_pos = _tile_bias_is_const(qmin_ref, qmax_ref, kmin_ref, kmax_ref,
                                              qi, jnp.maximum(j - 1, 0))
    nb = REL_BUCKETS // 2

    def bias_scalar():
        return LOG2E * jnp.where(all_pos, tab_smem[REL_BUCKETS - 1, h], tab_smem[nb - 1, h])

    def bias_tile():
        tk, tq = bkt_ref.shape
        row = jnp.broadcast_to(tab_rows[pl.ds(h, 1), :], (tk, LANES)) * LOG2E
        return jnp.concatenate(
            [jnp.take_along_axis(row, bkt_ref[:, c:c + LANES].astype(jnp.int32), axis=1)
             for c in range(0, tq, LANES)], axis=1)

    def finalize():
        lam = (jnp.exp(jnp.sum(lq1_ref[...] * lk1_ref[...], axis=-1, keepdims=True))
               - jnp.exp(jnp.sum(lq2_ref[...] * lk2_ref[...], axis=-1, keepdims=True))
               + LAMBDA_INIT)
        ot = acc1_ref[...] / l1_ref[...] - lam * (acc2_ref[...] / l2_ref[...])
        o = _rms(ot.T, g_ref[...]) * (1.0 - LAMBDA_INIT)
        o_ref[...] = o.astype(o_ref.dtype)

    @pl.when(j == 0)
    def _():
        m1_ref[...] = jnp.full_like(m1_ref, -jnp.inf)
        m2_ref[...] = jnp.full_like(m2_ref, -jnp.inf)
        l1_ref[...] = jnp.zeros_like(l1_ref)
        l2_ref[...] = jnp.zeros_like(l2_ref)
        acc1_ref[...] = jnp.zeros_like(acc1_ref)
        acc2_ref[...] = jnp.zeros_like(acc2_ref)
        _attn_logits(maps, *slots[0])

    bias_kinds = ((const_bias, bias_scalar), (jnp.logical_not(const_bias), bias_tile))
    middle = jnp.logical_and(j > 0, j < nk)
    for parity in (0, 1):
        on_parity = jnp.logical_and(middle, j % 2 == parity)
        for is_kind, get_bias in bias_kinds:
            @pl.when(jnp.logical_and(on_parity, is_kind))
            def _(parity=parity, get_bias=get_bias):
                bias_l = get_bias()
                _attn_logits(maps, *slots[parity])
                _attn_accumulate(maps, *slots[1 - parity], vt_ref, bias_l)

    for is_kind, get_bias in bias_kinds:
        @pl.when(jnp.logical_and(j == nk, is_kind))
        def _(get_bias=get_bias):
            _attn_accumulate(maps, *slots[(nk - 1) % 2], vt_ref, get_bias())
            finalize()


def _diff_attention(qt, k, vt, positions, rel_table, lq1, lk1, lq2, lk2, subln_g, tq, tk):
    S = k.shape[0]
    H = N_HEADS
    nq, nk = S // tq, S // tk
    pos = positions.reshape(S).astype(jnp.int32)
    qmin = pos.reshape(nq, tq).min(axis=1)
    qmax = pos.reshape(nq, tq).max(axis=1)
    kmin = pos.reshape(nk, tk).min(axis=1)
    kmax = pos.reshape(nk, tk).max(axis=1)
    tab_rows = jnp.zeros((H, LANES), F32).at[:, :REL_BUCKETS].set(rel_table.T)

    buckets = pl.pallas_call(
        _bucket_kernel,
        grid_spec=pltpu.PrefetchScalarGridSpec(
            num_scalar_prefetch=4,
            grid=(nk, nq),
            in_specs=[pl.BlockSpec((1, tq), lambda ki, qi, *_: (0, qi)),
                      pl.BlockSpec((tk, 1), lambda ki, qi, *_: (ki, 0))],
            out_specs=pl.BlockSpec((tk, tq), lambda ki, qi, *_: (ki, qi)),
        ),
        out_shape=jax.ShapeDtypeStruct((S, S), jnp.int8),
        compiler_params=_cparams(("parallel", "arbitrary")),
        name="rel_buckets",
    )(qmin, qmax, kmin, kmax, pos.reshape(1, S), pos.reshape(S, 1))

    per_elem = jnp.logical_not(jnp.logical_or(
        kmin[None, :] - qmax[:, None] >= REL_MAX_DIST,
        kmax[None, :] - qmin[:, None] <= -REL_MAX_DIST))
    tile_of_step = jnp.maximum(jnp.arange(nk + 1, dtype=jnp.int32) - 1, 0)
    wanted = jnp.where(per_elem[:, tile_of_step], tile_of_step[None, :], -1)
    held = lax.cummax(wanted, axis=1)
    first = jnp.argmax(per_elem, axis=1).astype(jnp.int32)
    bkt_blk = jnp.where(held >= 0, held, first[:, None]).reshape(-1)

    vec = lambda n: pl.BlockSpec((1, n), lambda h, qi, j, *_: (0, 0))
    cur = lambda j: jnp.minimum(j, nk - 1)
    prev = lambda j: jnp.maximum(j - 1, 0)
    in_specs = [
        pl.BlockSpec(memory_space=pltpu.SMEM),
        pl.BlockSpec((H, LANES), lambda h, qi, j, *_: (0, 0)),
        pl.BlockSpec((tk, tq),
                     lambda h, qi, j, a, b, c, d, blk: (blk[qi * (nk + 1) + j], qi)),
        vec(HEAD_DIM), vec(HEAD_DIM), vec(HEAD_DIM), vec(HEAD_DIM), vec(V_DIM),
        pl.BlockSpec((2 * HEAD_DIM, tq), lambda h, qi, j, *_: (h, qi)),
        pl.BlockSpec((tk, 2 * HEAD_DIM), lambda h, qi, j, *_: (cur(j), h)),
        pl.BlockSpec((V_DIM, tk), lambda h, qi, j, *_: (h, prev(j))),
    ]
    state = [pltpu.VMEM((1, tq), F32), pltpu.VMEM((1, tq), F32), pltpu.VMEM((V_DIM, tq), F32)]
    slot = [pltpu.VMEM((tk, tq), F32), pltpu.VMEM((tk, tq), F32),
            pltpu.VMEM((1, tq), F32), pltpu.VMEM((1, tq), F32)]
    grid_spec = pltpu.PrefetchScalarGridSpec(
        num_scalar_prefetch=5,
        grid=(H, nq, nk + 1),
        in_specs=in_specs,
        out_specs=pl.BlockSpec((tq, V_DIM), lambda h, qi, j, *_: (qi, h)),
        scratch_shapes=state + state + slot + slot,
    )
    return pl.pallas_call(
        functools.partial(_attn_kernel, nk=nk),
        grid_spec=grid_spec,
        out_shape=jax.ShapeDtypeStruct((S, H * V_DIM), BF16),
        compiler_params=_cparams(("parallel", "parallel", "arbitrary")),
        name="diff_attn",
    )(qmin, qmax, kmin, kmax, bkt_blk,
      rel_table, tab_rows, buckets,
      lq1.reshape(1, -1), lk1.reshape(1, -1), lq2.reshape(1, -1), lk2.reshape(1, -1),
      subln_g.reshape(1, -1),
      qt, k, vt)


HALO = 2 * SUBLANES
CONV_COLS_CHUNK = 2 * LANES
CONV_ROWS_CHUNK = 64


def _conv_kernel(prev_ref, cur_ref, next_ref, w_ref, b_ref, g_ref, beta_ref,
                 o_ref, buf_ref, shift_ref, conv_ref):
    i = pl.program_id(0)
    ts, C = cur_ref.shape
    buf_ref[0:HALO, :] = jnp.where(i > 0, prev_ref[...], 0.0)
    buf_ref[HALO:HALO + ts, :] = cur_ref[...]
    buf_ref[HALO + ts:, :] = jnp.where(i < pl.num_programs(0) - 1, next_ref[...], 0.0)
    n_shift = shift_ref.shape[1]

    def col_body(cc, carry):
        c0 = pl.multiple_of(cc * CONV_COLS_CHUNK, CONV_COLS_CHUNK)
        cols = pl.ds(c0, CONV_COLS_CHUNK)
        for r in range(1, SUBLANES):
            shift_ref[r - 1] = buf_ref[r:r + n_shift, cols]
        for r0 in range(0, ts, CONV_ROWS_CHUNK):
            acc = jnp.broadcast_to(b_ref[:, cols], (CONV_ROWS_CHUNK, CONV_COLS_CHUNK))
            for j in range(CONV_WIDTH):
                start = HALO - CONV_PAD + j
                r, base = start % SUBLANES, start - start % SUBLANES + r0
                if r == 0:
                    taps = buf_ref[base:base + CONV_ROWS_CHUNK, cols]
                else:
                    taps = shift_ref[r - 1, base:base + CONV_ROWS_CHUNK, :]
                acc = acc + w_ref[j:j + 1, cols] * taps
            conv_ref[r0:r0 + CONV_ROWS_CHUNK, cols] = acc
        return carry

    lax.fori_loop(0, C // CONV_COLS_CHUNK, col_body, 0)

    c = conv_ref[...]
    mu = jnp.mean(c, axis=-1, keepdims=True)
    d = c - mu
    var = jnp.mean(d * d, axis=-1, keepdims=True)
    y = d * lax.rsqrt(var + EPS) * g_ref[...] + beta_ref[...]
    o_ref[...] = (y * _sigmoid(y)).astype(o_ref.dtype)


def _conv_module(c, w_dw, b_dw, ln_g, ln_b, ts=256):
    S, C = c.shape
    hb = ts // HALO
    last = S // HALO - 1
    vec = pl.BlockSpec((1, C), lambda i: (0, 0))
    return pl.pallas_call(
        _conv_kernel,
        grid=(S // ts,),
        in_specs=[
            pl.BlockSpec((HALO, C), lambda i: (jnp.maximum(i * hb - 1, 0), 0)),
            pl.BlockSpec((ts, C), lambda i: (i, 0)),
            pl.BlockSpec((HALO, C), lambda i: (jnp.minimum((i + 1) * hb, last), 0)),
            pl.BlockSpec((CONV_WIDTH, C), lambda i: (0, 0)),
            vec, vec, vec,
        ],
        out_specs=pl.BlockSpec((ts, C), lambda i: (i, 0)),
        out_shape=jax.ShapeDtypeStruct((S, C), BF16),
        scratch_shapes=[pltpu.VMEM((ts + 2 * HALO, C), F32),
                        pltpu.VMEM((SUBLANES - 1, ts + 2 * HALO - SUBLANES, CONV_COLS_CHUNK), F32),
                        pltpu.VMEM((ts, C), F32)],
        compiler_params=_cparams(("parallel",)),
        name="conv_module",
    )(c, c, c, w_dw.reshape(CONV_WIDTH, C), b_dw.reshape(1, C),
      ln_g.reshape(1, C), ln_b.reshape(1, C))


def kernel(x, p, positions, rel_table, mix_pre_g, w_in, lambda_q1, lambda_k1, lambda_q2, lambda_k2, subln_g, w_attn_o, w_dw, b_dw, conv_ln_g, conv_ln_b, w_conv_o, w_out, mix_post_g, ffn_pre_g, w_up, w_down, ffn_post_g, w_ple_gate, w_ple_proj, ple_post_g):
    B, S, D = x.shape
    assert B == 1 and w_in.shape[0] == 1
    x2d = x.reshape(S, D)
    qkv_cols = 3 * N_HEADS * 2 * HEAD_DIM
    conv_off = qkv_cols
    gate_off = qkv_cols + 2 * D

    w_attn_o_b = w_attn_o[0].astype(BF16)
    w_conv_o_b = w_conv_o[0].astype(BF16)
    p_b = p[0, 0].astype(BF16)

    tall = dict(tm=2048, tn=512, single_buffer_a=True)
    tall_two_weights = dict(tm=2048, tn=256, single_buffer_a=True)

    h = _prenorm(x2d, mix_pre_g[0])
    head_cols = N_HEADS * 2 * HEAD_DIM
    qt = _fused_mm([(h, w_in[0], 0)], [], lambda a: a, head_cols, BF16, name="q_proj",
                   transpose_out=True, **tall)
    k = _fused_mm([(h, w_in[0], head_cols)], [], lambda a: a, head_cols, BF16, name="k_proj",
                  **tall)
    vt = _fused_mm([(h, w_in[0], 2 * head_cols)], [], lambda a: a, head_cols, BF16,
                   name="v_proj", transpose_out=True, **tall)
    glu = _fused_mm([(h, w_in[0], conv_off), (h, w_in[0], conv_off + D)], [],
                    lambda a, b: a * _sigmoid(b), D, F32, name="conv_glu", **tall_two_weights)
    gates = _fused_mm([(h, w_in[0], gate_off)], [], _sigmoid, 2 * D, F32,
                      name="merge_gates", **tall)

    o = _diff_attention(qt, k, vt, positions, rel_table, lambda_q1[0], lambda_k1[0],
                        lambda_q2[0], lambda_k2[0], subln_g[0], tq=512, tk=1024)
    c = _conv_module(glu, w_dw[0], b_dw[0], conv_ln_g[0], conv_ln_b[0])

    merged = _fused_mm([(o, w_attn_o_b, 0), (c, w_conv_o_b, 0)],
                       [(gates, 0), (gates, D)],
                       lambda ya, yc, ga, gc: ga * ya + gc * yc,
                       D, BF16, tm=1024, tn=512, name="branch_merge", single_buffer_a=True)
    mix = _fused_mm([(merged, w_out[0], 0)], [], lambda a: a, D, F32, name="mix_out", **tall)
    x1, h2 = _postnorm_res(x2d, mix, mix_post_g[0], ffn_pre_g[0])

    u = _fused_mm([(h2, w_up[0], 0)], [], lambda a: jnp.square(jnp.maximum(a, 0.0)),
                  w_up.shape[2], BF16, name="ffn_up", **tall)
    dn = _mm_ktiled(u, w_down[0], tm=2048, tn=1024, tk=1024, name="ffn_down")
    x2, x2_b = _postnorm_res(x1, dn, ffn_post_g[0])

    ple = _fused_mm([(x2_b, w_ple_gate[0], 0), (p_b, w_ple_proj[0], 0)], [],
                    lambda a, e: _sigmoid(a) * e, D, F32, name="ple_gate", **tall)
    out = _final_res(x2, ple, ple_post_g[0])
    return out.reshape(B, S, D)
```

```python
import functools
import math

import jax
import jax.numpy as jnp
from jax import lax
from jax.experimental import pallas as pl
from jax.experimental.pallas import tpu as pltpu

N_HEADS = 16
HEAD_DIM = 128
V_DIM = 2 * HEAD_DIM
CONV_WIDTH = 31
CONV_PAD = CONV_WIDTH // 2
REL_BUCKETS = 32
REL_MAX_DIST = 128
EPS = 1e-6
LAMBDA_INIT = 0.8 - 0.6 * math.exp(-0.3 * 0)

LANES = 128
SUBLANES = 8
VMEM_LIMIT_BYTES = 56 * 1024 * 1024

F32 = jnp.float32
BF16 = jnp.bfloat16


def _cparams(sem, flags=None):
    return pltpu.CompilerParams(dimension_semantics=sem,
                                vmem_limit_bytes=VMEM_LIMIT_BYTES, flags=flags)


def _sigmoid(x):
    return 1.0 / (1.0 + jnp.exp(-x))


def _rms(x, g):
    return x * lax.rsqrt(jnp.mean(x * x, axis=-1, keepdims=True) + EPS) * g


def _prenorm_kernel(x_ref, g_ref, h_ref):
    h_ref[...] = _rms(x_ref[...], g_ref[...]).astype(h_ref.dtype)


def _prenorm(x, g, tr=256):
    S, D = x.shape
    return pl.pallas_call(
        _prenorm_kernel,
        grid=(S // tr,),
        in_specs=[pl.BlockSpec((tr, D), lambda i: (i, 0)),
                  pl.BlockSpec((1, D), lambda i: (0, 0))],
        out_specs=pl.BlockSpec((tr, D), lambda i: (i, 0)),
        out_shape=jax.ShapeDtypeStruct((S, D), BF16),
        compiler_params=_cparams(("parallel",)),
        name="prenorm",
    )(x, g.reshape(1, D))


def _postnorm_res_kernel(x_ref, y_ref, gpost_ref, *rest, with_prenorm):
    x_new = x_ref[...] + _rms(y_ref[...], gpost_ref[...])
    if with_prenorm:
        gpre_ref, xo_ref, h_ref = rest
        xo_ref[...] = x_new
        h_ref[...] = _rms(x_new, gpre_ref[...]).astype(h_ref.dtype)
    else:
        xo_ref, h_ref = rest
        xo_ref[...] = x_new
        h_ref[...] = x_new.astype(h_ref.dtype)


def _postnorm_res(x, y, g_post, g_pre=None, tr=256):
    S, D = x.shape
    row = pl.BlockSpec((tr, D), lambda i: (i, 0))
    vec = pl.BlockSpec((1, D), lambda i: (0, 0))
    args = [x, y, g_post.reshape(1, D)]
    in_specs = [row, row, vec]
    if g_pre is not None:
        args.append(g_pre.reshape(1, D))
        in_specs.append(vec)
    return pl.pallas_call(
        functools.partial(_postnorm_res_kernel, with_prenorm=g_pre is not None),
        grid=(S // tr,),
        in_specs=in_specs,
        out_specs=[row, row],
        out_shape=[jax.ShapeDtypeStruct((S, D), F32),
                   jax.ShapeDtypeStruct((S, D), BF16)],
        compiler_params=_cparams(("parallel",)),
        name="postnorm_res",
    )(*args)


def _final_kernel(x_ref, y_ref, g_ref, o_ref):
    o_ref[...] = x_ref[...] + _rms(y_ref[...], g_ref[...])


def _final_res(x, y, g, tr=256):
    S, D = x.shape
    row = pl.BlockSpec((tr, D), lambda i: (i, 0))
    return pl.pallas_call(
        _final_kernel,
        grid=(S // tr,),
        in_specs=[row, row, pl.BlockSpec((1, D), lambda i: (0, 0))],
        out_specs=row,
        out_shape=jax.ShapeDtypeStruct((S, D), F32),
        compiler_params=_cparams(("parallel",)),
        name="final_res",
    )(x, y, g.reshape(1, D))


def _fused_mm_kernel(*refs, a_index, epilogue, transpose_out):
    n_a = max(a_index) + 1
    n_pairs = len(a_index)
    out_ref = refs[-1]
    accs = [jnp.dot(refs[a_index[i]][...], refs[n_a + i][...].astype(BF16),
                    preferred_element_type=F32) for i in range(n_pairs)]
    extras = [r[...] for r in refs[n_a + n_pairs:-1]]
    out = epilogue(*accs, *extras)
    out_ref[...] = (out.T if transpose_out else out).astype(out_ref.dtype)


def _fused_mm(pairs, extras, epilogue, n_out, out_dtype, tm, tn, name, single_buffer_a=False,
              transpose_out=False):
    M = pairs[0][0].shape[0]
    a_list, a_index = [], []
    for a, _, _ in pairs:
        for idx, seen in enumerate(a_list):
            if seen is a:
                a_index.append(idx)
                break
        else:
            a_index.append(len(a_list))
            a_list.append(a)
    a_mode = dict(pipeline_mode=pl.Buffered(1)) if single_buffer_a else {}
    in_specs = [pl.BlockSpec((tm, a.shape[1]), lambda m, n: (m, 0), **a_mode) for a in a_list]
    args = list(a_list)
    for a, w, off in pairs:
        ob = off // tn
        in_specs.append(pl.BlockSpec((a.shape[1], tn), lambda m, n, ob=ob: (0, n + ob)))
        args.append(w)
    for e, off in extras:
        ob = off // tn
        in_specs.append(pl.BlockSpec((tm, tn), lambda m, n, ob=ob: (m, n + ob)))
        args.append(e)
    return pl.pallas_call(
        functools.partial(_fused_mm_kernel, a_index=tuple(a_index), epilogue=epilogue,
                          transpose_out=transpose_out),
        grid=(M // tm, n_out // tn),
        in_specs=in_specs,
        out_specs=(pl.BlockSpec((tn, tm), lambda m, n: (n, m)) if transpose_out
                   else pl.BlockSpec((tm, tn), lambda m, n: (m, n))),
        out_shape=jax.ShapeDtypeStruct((n_out, M) if transpose_out else (M, n_out), out_dtype),
        compiler_params=_cparams(("parallel", "arbitrary")),
        name=name,
    )(*args)


def _mm_ktiled_kernel(a_ref, w_ref, o_ref):
    k = pl.program_id(2)

    def part():
        return jnp.dot(a_ref[...], w_ref[...].astype(BF16), preferred_element_type=F32)

    @pl.when(k == 0)
    def _():
        o_ref[...] = part()

    @pl.when(k > 0)
    def _():
        o_ref[...] += part()


def _mm_ktiled(a, w, tm, tn, tk, name):
    M, K = a.shape
    N = w.shape[1]
    return pl.pallas_call(
        _mm_ktiled_kernel,
        grid=(M // tm, N // tn, K // tk),
        in_specs=[pl.BlockSpec((tm, tk), lambda m, n, k: (m, k)),
                  pl.BlockSpec((tk, tn), lambda m, n, k: (k, n))],
        out_specs=pl.BlockSpec((tm, tn), lambda m, n, k: (m, n)),
        out_shape=jax.ShapeDtypeStruct((M, N), F32),
        compiler_params=_cparams(("parallel", "parallel", "arbitrary")),
        name=name,
    )(a, w)


def _rel_bucket(rel):
    nb = REL_BUCKETS // 2
    max_exact = nb // 2
    ret = jnp.where(rel > 0, nb, 0)
    n = jnp.abs(rel)
    nf = jnp.maximum(n, 1).astype(F32)
    large = max_exact + (jnp.log(nf / max_exact) / math.log(REL_MAX_DIST / max_exact)
                         * (nb - max_exact)).astype(jnp.int32)
    large = jnp.minimum(large, nb - 1)
    return ret + jnp.where(n < max_exact, n, large)


LOG2E = math.log2(math.e)
LOGIT_SCALE_LOG2 = (HEAD_DIM ** -0.5) * LOG2E
ATTN_KEYS_CHUNK = 256


def _attn_logits(maps, s_scrs, r_scrs, bias_l=None):
    logits = [jnp.dot(k_ref[...], qt_ref[...], preferred_element_type=F32)
              for qt_ref, k_ref, _, _, _ in maps]
    for s, s_scr, r_scr in zip(logits, s_scrs, r_scrs):
        s = s * LOGIT_SCALE_LOG2
        if bias_l is not None:
            s = s + bias_l
        s_scr[...] = s
        r_scr[...] = jnp.max(s, axis=0, keepdims=True)


def _attn_accumulate(maps, s_scrs, r_scrs, vt_ref, bias_l):
    tk = vt_ref.shape[1]
    rows = []
    for (_, _, m_ref, _, _), r_scr in zip(maps, r_scrs):
        m_prev = m_ref[...]
        m_new = jnp.maximum(m_prev, r_scr[...] + bias_l)
        shift = bias_l - m_new
        m_ref[...] = m_new
        rows.append((jnp.exp2(m_prev - m_new), shift))
    sums = [None, None]
    probs = [[], []]
    for c in range(0, tk, ATTN_KEYS_CHUNK):
        for i, ((_, shift), s_scr) in enumerate(zip(rows, s_scrs)):
            s = s_scr[c:c + ATTN_KEYS_CHUNK, :]
            p = jnp.exp2(s + shift)
            psum = jnp.sum(p, axis=0, keepdims=True)
            sums[i] = psum if sums[i] is None else sums[i] + psum
            probs[i].append(p.astype(vt_ref.dtype))
    for (alpha, _), psum, blocks, (_, _, _, l_ref, acc_ref) in zip(rows, sums, probs, maps):
        l_ref[...] = alpha * l_ref[...] + psum
        acc_ref[...] = alpha * acc_ref[...] + jnp.dot(
            vt_ref[...], jnp.concatenate(blocks, axis=0), preferred_element_type=F32)


def _bucket_kernel(qmin_ref, qmax_ref, kmin_ref, kmax_ref, posq_ref, posk_ref, o_ref):
    ki = pl.program_id(0)
    qi = pl.program_id(1)
    const_bias = _tile_bias_is_const(qmin_ref, qmax_ref, kmin_ref, kmax_ref, qi, ki)[0]

    @pl.when(const_bias)
    def _():
        o_ref[...] = jnp.zeros_like(o_ref)

    @pl.when(jnp.logical_not(const_bias))
    def _():
        o_ref[...] = _rel_bucket(posk_ref[...] - posq_ref[...]).astype(o_ref.dtype)


def _tile_bias_is_const(qmin_ref, qmax_ref, kmin_ref, kmax_ref, qi, ki):
    all_pos = kmin_ref[ki] - qmax_ref[qi] >= REL_MAX_DIST
    all_neg = kmax_ref[ki] - qmin_ref[qi] <= -REL_MAX_DIST
    return jnp.logical_or(all_pos, all_neg), all_pos


def _attn_kernel(qmin_ref, qmax_ref, kmin_ref, kmax_ref, bkt_blk_ref,
                 tab_smem, tab_rows, bkt_ref,
                 lq1_ref, lk1_ref, lq2_ref, lk2_ref, g_ref,
                 qt_ref, k_ref, vt_ref,
                 o_ref,
                 m1_ref, l1_ref, acc1_ref, m2_ref, l2_ref, acc2_ref,
                 s1a_scr, s2a_scr, r1a_scr, r2a_scr,
                 s1b_scr, s2b_scr, r1b_scr, r2b_scr, *, nk):
    h = pl.program_id(0)
    qi = pl.program_id(1)
    j = pl.program_id(2)
    lo, hi = pl.ds(0, HEAD_DIM), pl.ds(HEAD_DIM, HEAD_DIM)
    maps = [(qt_ref.at[lo, :], k_ref.at[:, lo], m1_ref, l1_ref, acc1_ref),
            (qt_ref.at[hi, :], k_ref.at[:, hi], m2_ref, l2_ref, acc2_ref)]
    slots = [((s1a_scr, s2a_scr), (r1a_scr, r2a_scr)),
             ((s1b_scr, s2b_scr), (r1b_scr, r2b_scr))]

    prev_const, all_pos = _tile_bias_is_const(qmin_ref, qmax_ref, kmin_ref, kmax_ref,
                                              qi, jnp.maximum(j - 1, 0))
    cur_const = _tile_bias_is_const(qmin_ref, qmax_ref, kmin_ref, kmax_ref,
                                    qi, jnp.minimum(j, nk - 1))[0]
    nb = REL_BUCKETS // 2

    def bias_scalar():
        entry = jnp.where(all_pos, tab_smem[REL_BUCKETS - 1, h], tab_smem[nb - 1, h])
        return LOG2E * jnp.where(prev_const, entry, 0.0)

    def bias_tile():
        tk, tq = bkt_ref.shape
        row = jnp.broadcast_to(tab_rows[pl.ds(h, 1), :], (tk, LANES)) * LOG2E
        return jnp.concatenate(
            [jnp.take_along_axis(row, bkt_ref[:, c:c + LANES].astype(jnp.int32), axis=1)
             for c in range(0, tq, LANES)], axis=1)

    def finalize():
        lam = (jnp.exp(jnp.sum(lq1_ref[...] * lk1_ref[...], axis=-1, keepdims=True))
               - jnp.exp(jnp.sum(lq2_ref[...] * lk2_ref[...], axis=-1, keepdims=True))
               + LAMBDA_INIT)
        ot = acc1_ref[...] / l1_ref[...] - lam * (acc2_ref[...] / l2_ref[...])
        o = _rms(ot.T, g_ref[...]) * (1.0 - LAMBDA_INIT)
        o_ref[...] = o.astype(o_ref.dtype)

    cur_kinds = ((cur_const, lambda: None), (jnp.logical_not(cur_const), bias_tile))

    for is_cur, cur_bias in cur_kinds:
        @pl.when(jnp.logical_and(j == 0, is_cur))
        def _(cur_bias=cur_bias):
            m1_ref[...] = jnp.full_like(m1_ref, -jnp.inf)
            m2_ref[...] = jnp.full_like(m2_ref, -jnp.inf)
            l1_ref[...] = jnp.zeros_like(l1_ref)
            l2_ref[...] = jnp.zeros_like(l2_ref)
            acc1_ref[...] = jnp.zeros_like(acc1_ref)
            acc2_ref[...] = jnp.zeros_like(acc2_ref)
            _attn_logits(maps, *slots[0], cur_bias())

    middle = jnp.logical_and(j > 0, j < nk)
    for parity in (0, 1):
        on_parity = jnp.logical_and(middle, j % 2 == parity)
        for is_cur, cur_bias in cur_kinds:
            @pl.when(jnp.logical_and(on_parity, is_cur))
            def _(parity=parity, cur_bias=cur_bias):
                _attn_logits(maps, *slots[parity], cur_bias())
                _attn_accumulate(maps, *slots[1 - parity], vt_ref, bias_scalar())

    @pl.when(j == nk)
    def _():
        _attn_accumulate(maps, *slots[(nk - 1) % 2], vt_ref, bias_scalar())
        finalize()


def _diff_attention(qt, k, vt, positions, rel_table, lq1, lk1, lq2, lk2, subln_g, tq, tk):
    S = k.shape[0]
    H = N_HEADS
    nq, nk = S // tq, S // tk
    pos = positions.reshape(S).astype(jnp.int32)
    qmin = pos.reshape(nq, tq).min(axis=1)
    qmax = pos.reshape(nq, tq).max(axis=1)
    kmin = pos.reshape(nk, tk).min(axis=1)
    kmax = pos.reshape(nk, tk).max(axis=1)
    tab_rows = jnp.zeros((H, LANES), F32).at[:, :REL_BUCKETS].set(rel_table.T)

    buckets = pl.pallas_call(
        _bucket_kernel,
        grid_spec=pltpu.PrefetchScalarGridSpec(
            num_scalar_prefetch=4,
            grid=(nk, nq),
            in_specs=[pl.BlockSpec((1, tq), lambda ki, qi, *_: (0, qi)),
                      pl.BlockSpec((tk, 1), lambda ki, qi, *_: (ki, 0))],
            out_specs=pl.BlockSpec((tk, tq), lambda ki, qi, *_: (ki, qi)),
        ),
        out_shape=jax.ShapeDtypeStruct((S, S), jnp.int8),
        compiler_params=_cparams(("parallel", "arbitrary")),
        name="rel_buckets",
    )(qmin, qmax, kmin, kmax, pos.reshape(1, S), pos.reshape(S, 1))

    per_elem = jnp.logical_not(jnp.logical_or(
        kmin[None, :] - qmax[:, None] >= REL_MAX_DIST,
        kmax[None, :] - qmin[:, None] <= -REL_MAX_DIST))
    tile_of_step = jnp.minimum(jnp.arange(nk + 1, dtype=jnp.int32), nk - 1)
    wanted = jnp.where(per_elem[:, tile_of_step], tile_of_step[None, :], -1)
    held = lax.cummax(wanted, axis=1)
    first = jnp.argmax(per_elem, axis=1).astype(jnp.int32)
    bkt_blk = jnp.where(held >= 0, held, first[:, None]).reshape(-1)

    vec = lambda n: pl.BlockSpec((1, n), lambda h, qi, j, *_: (0, 0))
    cur = lambda j: jnp.minimum(j, nk - 1)
    prev = lambda j: jnp.maximum(j - 1, 0)
    in_specs = [
        pl.BlockSpec(memory_space=pltpu.SMEM),
        pl.BlockSpec((H, LANES), lambda h, qi, j, *_: (0, 0)),
        pl.BlockSpec((tk, tq),
                     lambda h, qi, j, a, b, c, d, blk: (blk[qi * (nk + 1) + j], qi)),
        vec(HEAD_DIM), vec(HEAD_DIM), vec(HEAD_DIM), vec(HEAD_DIM), vec(V_DIM),
        pl.BlockSpec((2 * HEAD_DIM, tq), lambda h, qi, j, *_: (h, qi)),
        pl.BlockSpec((tk, 2 * HEAD_DIM), lambda h, qi, j, *_: (cur(j), h)),
        pl.BlockSpec((V_DIM, tk), lambda h, qi, j, *_: (h, prev(j))),
    ]
    state = [pltpu.VMEM((1, tq), F32), pltpu.VMEM((1, tq), F32), pltpu.VMEM((V_DIM, tq), F32)]
    slot = [pltpu.VMEM((tk, tq), F32), pltpu.VMEM((tk, tq), F32),
            pltpu.VMEM((1, tq), F32), pltpu.VMEM((1, tq), F32)]
    grid_spec = pltpu.PrefetchScalarGridSpec(
        num_scalar_prefetch=5,
        grid=(H, nq, nk + 1),
        in_specs=in_specs,
        out_specs=pl.BlockSpec((tq, V_DIM), lambda h, qi, j, *_: (qi, h)),
        scratch_shapes=state + state + slot + slot,
    )
    return pl.pallas_call(
        functools.partial(_attn_kernel, nk=nk),
        grid_spec=grid_spec,
        out_shape=jax.ShapeDtypeStruct((S, H * V_DIM), BF16),
        compiler_params=_cparams(("parallel", "parallel", "arbitrary")),
        name="diff_attn",
    )(qmin, qmax, kmin, kmax, bkt_blk,
      rel_table, tab_rows, buckets,
      lq1.reshape(1, -1), lk1.reshape(1, -1), lq2.reshape(1, -1), lk2.reshape(1, -1),
      subln_g.reshape(1, -1),
      qt, k, vt)


HALO = 2 * SUBLANES
CONV_COLS_CHUNK = 2 * LANES
CONV_ROWS_CHUNK = 64


def _conv_kernel(prev_ref, cur_ref, next_ref, w_ref, b_ref, g_ref, beta_ref,
                 o_ref, buf_ref, shift_ref, conv_ref):
    i = pl.program_id(0)
    ts, C = cur_ref.shape
    buf_ref[0:HALO, :] = jnp.where(i > 0, prev_ref[...], 0.0)
    buf_ref[HALO:HALO + ts, :] = cur_ref[...]
    buf_ref[HALO + ts:, :] = jnp.where(i < pl.num_programs(0) - 1, next_ref[...], 0.0)
    n_shift = shift_ref.shape[1]

    def col_body(cc, carry):
        c0 = pl.multiple_of(cc * CONV_COLS_CHUNK, CONV_COLS_CHUNK)
        cols = pl.ds(c0, CONV_COLS_CHUNK)
        for r in range(1, SUBLANES):
            shift_ref[r - 1] = buf_ref[r:r + n_shift, cols]
        for r0 in range(0, ts, CONV_ROWS_CHUNK):
            acc = jnp.broadcast_to(b_ref[:, cols], (CONV_ROWS_CHUNK, CONV_COLS_CHUNK))
            for j in range(CONV_WIDTH):
                start = HALO - CONV_PAD + j
                r, base = start % SUBLANES, start - start % SUBLANES + r0
                if r == 0:
                    taps = buf_ref[base:base + CONV_ROWS_CHUNK, cols]
                else:
                    taps = shift_ref[r - 1, base:base + CONV_ROWS_CHUNK, :]
                acc = acc + w_ref[j:j + 1, cols] * taps
            conv_ref[r0:r0 + CONV_ROWS_CHUNK, cols] = acc
        return carry

    lax.fori_loop(0, C // CONV_COLS_CHUNK, col_body, 0)

    c = conv_ref[...]
    mu = jnp.mean(c, axis=-1, keepdims=True)
    d = c - mu
    var = jnp.mean(d * d, axis=-1, keepdims=True)
    y = d * lax.rsqrt(var + EPS) * g_ref[...] + beta_ref[...]
    o_ref[...] = (y * _sigmoid(y)).astype(o_ref.dtype)


def _conv_module(c, w_dw, b_dw, ln_g, ln_b, ts=256):
    S, C = c.shape
    hb = ts // HALO
    last = S // HALO - 1
    vec = pl.BlockSpec((1, C), lambda i: (0, 0))
    return pl.pallas_call(
        _conv_kernel,
        grid=(S // ts,),
        in_specs=[
            pl.BlockSpec((HALO, C), lambda i: (jnp.maximum(i * hb - 1, 0), 0)),
            pl.BlockSpec((ts, C), lambda i: (i, 0)),
            pl.BlockSpec((HALO, C), lambda i: (jnp.minimum((i + 1) * hb, last), 0)),
            pl.BlockSpec((CONV_WIDTH, C), lambda i: (0, 0)),
            vec, vec, vec,
        ],
        out_specs=pl.BlockSpec((ts, C), lambda i: (i, 0)),
        out_shape=jax.ShapeDtypeStruct((S, C), BF16),
        scratch_shapes=[pltpu.VMEM((ts + 2 * HALO, C), F32),
                        pltpu.VMEM((SUBLANES - 1, ts + 2 * HALO - SUBLANES, CONV_COLS_CHUNK), F32),
                        pltpu.VMEM((ts, C), F32)],
        compiler_params=_cparams(("parallel",)),
        name="conv_module",
    )(c, c, c, w_dw.reshape(CONV_WIDTH, C), b_dw.reshape(1, C),
      ln_g.reshape(1, C), ln_b.reshape(1, C))


def kernel(x, p, positions, rel_table, mix_pre_g, w_in, lambda_q1, lambda_k1, lambda_q2, lambda_k2, subln_g, w_attn_o, w_dw, b_dw, conv_ln_g, conv_ln_b, w_conv_o, w_out, mix_post_g, ffn_pre_g, w_up, w_down, ffn_post_g, w_ple_gate, w_ple_proj, ple_post_g):
    B, S, D = x.shape
    assert B == 1 and w_in.shape[0] == 1
    x2d = x.reshape(S, D)
    qkv_cols = 3 * N_HEADS * 2 * HEAD_DIM
    conv_off = qkv_cols
    gate_off = qkv_cols + 2 * D

    w_attn_o_b = w_attn_o[0].astype(BF16)
    w_conv_o_b = w_conv_o[0].astype(BF16)
    p_b = p[0, 0].astype(BF16)

    tall = dict(tm=2048, tn=512, single_buffer_a=True)
    tall_two_weights = dict(tm=2048, tn=256, single_buffer_a=True)

    h = _prenorm(x2d, mix_pre_g[0])
    head_cols = N_HEADS * 2 * HEAD_DIM
    qt = _fused_mm([(h, w_in[0], 0)], [], lambda a: a, head_cols, BF16, name="q_proj",
                   transpose_out=True, **tall)
    k = _fused_mm([(h, w_in[0], head_cols)], [], lambda a: a, head_cols, BF16, name="k_proj",
                  **tall)
    vt = _fused_mm([(h, w_in[0], 2 * head_cols)], [], lambda a: a, head_cols, BF16,
                   name="v_proj", transpose_out=True, **tall)
    glu = _fused_mm([(h, w_in[0], conv_off), (h, w_in[0], conv_off + D)], [],
                    lambda a, b: a * _sigmoid(b), D, F32, name="conv_glu", **tall_two_weights)
    gates = _fused_mm([(h, w_in[0], gate_off)], [], _sigmoid, 2 * D, F32,
                      name="merge_gates", **tall)

    o = _diff_attention(qt, k, vt, positions, rel_table, lambda_q1[0], lambda_k1[0],
                        lambda_q2[0], lambda_k2[0], subln_g[0], tq=512, tk=1024)
    c = _conv_module(glu, w_dw[0], b_dw[0], conv_ln_g[0], conv_ln_b[0])

    merged = _fused_mm([(o, w_attn_o_b, 0), (c, w_conv_o_b, 0)],
                       [(gates, 0), (gates, D)],
                       lambda ya, yc, ga, gc: ga * ya + gc * yc,
                       D, BF16, tm=1024, tn=512, name="branch_merge", single_buffer_a=True)
    mix = _fused_mm([(merged, w_out[0], 0)], [], lambda a: a, D, F32, name="mix_out", **tall)
    x1, h2 = _postnorm_res(x2d, mix, mix_post_g[0], ffn_pre_g[0])

    u = _fused_mm([(h2, w_up[0], 0)], [], lambda a: jnp.square(jnp.maximum(a, 0.0)),
                  w_up.shape[2], BF16, name="ffn_up", **tall)
    dn = _mm_ktiled(u, w_down[0], tm=2048, tn=1024, tk=1024, name="ffn_down")
    x2, x2_b = _postnorm_res(x1, dn, ffn_post_g[0])

    ple = _fused_mm([(x2_b, w_ple_gate[0], 0), (p_b, w_ple_proj[0], 0)], [],
                    lambda a, e: _sigmoid(a) * e, D, F32, name="ple_gate", **tall)
    out = _final_res(x2, ple, ple_post_g[0])
    return out.reshape(B, S, D)
```

```python
import functools
import math

import jax
import jax.numpy as jnp
from jax import lax
from jax.experimental import pallas as pl
from jax.experimental.pallas import tpu as pltpu

N_HEADS = 16
HEAD_DIM = 128
V_DIM = 2 * HEAD_DIM
CONV_WIDTH = 31
CONV_PAD = CONV_WIDTH // 2
REL_BUCKETS = 32
REL_MAX_DIST = 128
EPS = 1e-6
LAMBDA_INIT = 0.8 - 0.6 * math.exp(-0.3 * 0)

LANES = 128
SUBLANES = 8
VMEM_LIMIT_BYTES = 56 * 1024 * 1024

F32 = jnp.float32
BF16 = jnp.bfloat16


def _cparams(sem, flags=None):
    return pltpu.CompilerParams(dimension_semantics=sem,
                                vmem_limit_bytes=VMEM_LIMIT_BYTES, flags=flags)


def _sigmoid(x):
    return 1.0 / (1.0 + jnp.exp(-x))


def _rms(x, g):
    return x * lax.rsqrt(jnp.mean(x * x, axis=-1, keepdims=True) + EPS) * g


def _prenorm_kernel(x_ref, g_ref, h_ref):
    h_ref[...] = _rms(x_ref[...], g_ref[...]).astype(h_ref.dtype)


def _prenorm(x, g, tr=256):
    S, D = x.shape
    return pl.pallas_call(
        _prenorm_kernel,
        grid=(S // tr,),
        in_specs=[pl.BlockSpec((tr, D), lambda i: (i, 0)),
                  pl.BlockSpec((1, D), lambda i: (0, 0))],
        out_specs=pl.BlockSpec((tr, D), lambda i: (i, 0)),
        out_shape=jax.ShapeDtypeStruct((S, D), BF16),
        compiler_params=_cparams(("parallel",)),
        name="prenorm",
    )(x, g.reshape(1, D))


def _postnorm_res_kernel(x_ref, y_ref, gpost_ref, *rest, with_prenorm):
    x_new = x_ref[...] + _rms(y_ref[...], gpost_ref[...])
    if with_prenorm:
        gpre_ref, xo_ref, h_ref = rest
        xo_ref[...] = x_new
        h_ref[...] = _rms(x_new, gpre_ref[...]).astype(h_ref.dtype)
    else:
        xo_ref, h_ref = rest
        xo_ref[...] = x_new
        h_ref[...] = x_new.astype(h_ref.dtype)


def _postnorm_res(x, y, g_post, g_pre=None, tr=256):
    S, D = x.shape
    row = pl.BlockSpec((tr, D), lambda i: (i, 0))
    vec = pl.BlockSpec((1, D), lambda i: (0, 0))
    args = [x, y, g_post.reshape(1, D)]
    in_specs = [row, row, vec]
    if g_pre is not None:
        args.append(g_pre.reshape(1, D))
        in_specs.append(vec)
    return pl.pallas_call(
        functools.partial(_postnorm_res_kernel, with_prenorm=g_pre is not None),
        grid=(S // tr,),
        in_specs=in_specs,
        out_specs=[row, row],
        out_shape=[jax.ShapeDtypeStruct((S, D), F32),
                   jax.ShapeDtypeStruct((S, D), BF16)],
        compiler_params=_cparams(("parallel",)),
        name="postnorm_res",
    )(*args)


def _final_kernel(x_ref, y_ref, g_ref, o_ref):
    o_ref[...] = x_ref[...] + _rms(y_ref[...], g_ref[...])


def _final_res(x, y, g, tr=256):
    S, D = x.shape
    row = pl.BlockSpec((tr, D), lambda i: (i, 0))
    return pl.pallas_call(
        _final_kernel,
        grid=(S // tr,),
        in_specs=[row, row, pl.BlockSpec((1, D), lambda i: (0, 0))],
        out_specs=row,
        out_shape=jax.ShapeDtypeStruct((S, D), F32),
        compiler_params=_cparams(("parallel",)),
        name="final_res",
    )(x, y, g.reshape(1, D))


def _fused_mm_kernel(*refs, a_index, epilogue, transpose_out):
    n_a = max(a_index) + 1
    n_pairs = len(a_index)
    out_ref = refs[-1]
    accs = [jnp.dot(refs[a_index[i]][...], refs[n_a + i][...].astype(BF16),
                    preferred_element_type=F32) for i in range(n_pairs)]
    extras = [r[...] for r in refs[n_a + n_pairs:-1]]
    out = epilogue(*accs, *extras)
    out_ref[...] = (out.T if transpose_out else out).astype(out_ref.dtype)


def _fused_mm(pairs, extras, epilogue, n_out, out_dtype, tm, tn, name, single_buffer_a=False,
              transpose_out=False):
    M = pairs[0][0].shape[0]
    a_list, a_index = [], []
    for a, _, _ in pairs:
        for idx, seen in enumerate(a_list):
            if seen is a:
                a_index.append(idx)
                break
        else:
            a_index.append(len(a_list))
            a_list.append(a)
    a_mode = dict(pipeline_mode=pl.Buffered(1)) if single_buffer_a else {}
    in_specs = [pl.BlockSpec((tm, a.shape[1]), lambda m, n: (m, 0), **a_mode) for a in a_list]
    args = list(a_list)
    for a, w, off in pairs:
        ob = off // tn
        in_specs.append(pl.BlockSpec((a.shape[1], tn), lambda m, n, ob=ob: (0, n + ob)))
        args.append(w)
    for e, off in extras:
        ob = off // tn
        in_specs.append(pl.BlockSpec((tm, tn), lambda m, n, ob=ob: (m, n + ob)))
        args.append(e)
    return pl.pallas_call(
        functools.partial(_fused_mm_kernel, a_index=tuple(a_index), epilogue=epilogue,
                          transpose_out=transpose_out),
        grid=(M // tm, n_out // tn),
        in_specs=in_specs,
        out_specs=(pl.BlockSpec((tn, tm), lambda m, n: (n, m)) if transpose_out
                   else pl.BlockSpec((tm, tn), lambda m, n: (m, n))),
        out_shape=jax.ShapeDtypeStruct((n_out, M) if transpose_out else (M, n_out), out_dtype),
        compiler_params=_cparams(("parallel", "arbitrary")),
        name=name,
    )(*args)


def _mm_ktiled_kernel(a_ref, w_ref, o_ref):
    k = pl.program_id(2)

    def part():
        return jnp.dot(a_ref[...], w_ref[...].astype(BF16), preferred_element_type=F32)

    @pl.when(k == 0)
    def _():
        o_ref[...] = part()

    @pl.when(k > 0)
    def _():
        o_ref[...] += part()


def _mm_ktiled(a, w, tm, tn, tk, name):
    M, K = a.shape
    N = w.shape[1]
    return pl.pallas_call(
        _mm_ktiled_kernel,
        grid=(M // tm, N // tn, K // tk),
        in_specs=[pl.BlockSpec((tm, tk), lambda m, n, k: (m, k)),
                  pl.BlockSpec((tk, tn), lambda m, n, k: (k, n))],
        out_specs=pl.BlockSpec((tm, tn), lambda m, n, k: (m, n)),
        out_shape=jax.ShapeDtypeStruct((M, N), F32),
        compiler_params=_cparams(("parallel", "parallel", "arbitrary")),
        name=name,
    )(a, w)


def _rel_bucket(rel):
    nb = REL_BUCKETS // 2
    max_exact = nb // 2
    ret = jnp.where(rel > 0, nb, 0)
    n = jnp.abs(rel)
    nf = jnp.maximum(n, 1).astype(F32)
    large = max_exact + (jnp.log(nf / max_exact) / math.log(REL_MAX_DIST / max_exact)
                         * (nb - max_exact)).astype(jnp.int32)
    large = jnp.minimum(large, nb - 1)
    return ret + jnp.where(n < max_exact, n, large)


LOG2E = math.log2(math.e)
LOGIT_SCALE_LOG2 = (HEAD_DIM ** -0.5) * LOG2E
ATTN_KEYS_CHUNK = 256


def _attn_logits(maps, s_scrs, r_scrs, bias_l=None):
    logits = [jnp.dot(k_ref[...], qt_ref[...], preferred_element_type=F32)
              for qt_ref, k_ref, _, _, _ in maps]
    for s, s_scr, r_scr in zip(logits, s_scrs, r_scrs):
        s = s * LOGIT_SCALE_LOG2
        if bias_l is not None:
            s = s + bias_l
        s_scr[...] = s
        r_scr[...] = jnp.max(s, axis=0, keepdims=True)


def _attn_accumulate(maps, s_scrs, r_scrs, vt_ref, bias_l):
    tk = vt_ref.shape[1]
    rows = []
    for (_, _, m_ref, _, _), r_scr in zip(maps, r_scrs):
        m_prev = m_ref[...]
        m_new = jnp.maximum(m_prev, r_scr[...] + bias_l)
        shift = bias_l - m_new
        m_ref[...] = m_new
        rows.append((jnp.exp2(m_prev - m_new), shift))
    sums = [None, None]
    probs = [[], []]
    for c in range(0, tk, ATTN_KEYS_CHUNK):
        for i, ((_, shift), s_scr) in enumerate(zip(rows, s_scrs)):
            s = s_scr[c:c + ATTN_KEYS_CHUNK, :]
            p = jnp.exp2(s + shift)
            psum = jnp.sum(p, axis=0, keepdims=True)
            sums[i] = psum if sums[i] is None else sums[i] + psum
            probs[i].append(p.astype(vt_ref.dtype))
    for (alpha, _), psum, blocks, (_, _, _, l_ref, acc_ref) in zip(rows, sums, probs, maps):
        l_ref[...] = alpha * l_ref[...] + psum
        acc_ref[...] = alpha * acc_ref[...] + jnp.dot(
            vt_ref[...], jnp.concatenate(blocks, axis=0), preferred_element_type=F32)


def _bucket_kernel(qmin_ref, qmax_ref, kmin_ref, kmax_ref, posq_ref, posk_ref, o_ref):
    ki = pl.program_id(0)
    qi = pl.program_id(1)
    const_bias = _tile_bias_is_const(qmin_ref, qmax_ref, kmin_ref, kmax_ref, qi, ki)[0]

    @pl.when(const_bias)
    def _():
        o_ref[...] = jnp.zeros_like(o_ref)

    @pl.when(jnp.logical_not(const_bias))
    def _():
        o_ref[...] = _rel_bucket(posk_ref[...] - posq_ref[...]).astype(o_ref.dtype)


def _tile_bias_is_const(qmin_ref, qmax_ref, kmin_ref, kmax_ref, qi, ki):
    all_pos = kmin_ref[ki] - qmax_ref[qi] >= REL_MAX_DIST
    all_neg = kmax_ref[ki] - qmin_ref[qi] <= -REL_MAX_DIST
    return jnp.logical_or(all_pos, all_neg), all_pos


def _attn_kernel(qmin_ref, qmax_ref, kmin_ref, kmax_ref, bkt_blk_ref,
                 tab_smem, tab_rows, bkt_ref, small_ref,
                 qt_ref, k_ref, vt_ref,
                 o_ref,
                 m1_ref, l1_ref, acc1_ref, m2_ref, l2_ref, acc2_ref,
                 s1a_scr, s2a_scr, r1a_scr, r2a_scr,
                 s1b_scr, s2b_scr, r1b_scr, r2b_scr, *, nk):
    h = pl.program_id(0)
    qi = pl.program_id(1)
    j = pl.program_id(2)
    lo, hi = pl.ds(0, HEAD_DIM), pl.ds(HEAD_DIM, HEAD_DIM)
    maps = [(qt_ref.at[lo, :], k_ref.at[:, lo], m1_ref, l1_ref, acc1_ref),
            (qt_ref.at[hi, :], k_ref.at[:, hi], m2_ref, l2_ref, acc2_ref)]
    slots = [((s1a_scr, s2a_scr), (r1a_scr, r2a_scr)),
             ((s1b_scr, s2b_scr), (r1b_scr, r2b_scr))]

    prev_const, all_pos = _tile_bias_is_const(qmin_ref, qmax_ref, kmin_ref, kmax_ref,
                                              qi, jnp.maximum(j - 1, 0))
    cur_const = _tile_bias_is_const(qmin_ref, qmax_ref, kmin_ref, kmax_ref,
                                    qi, jnp.minimum(j, nk - 1))[0]
    nb = REL_BUCKETS // 2

    def bias_scalar():
        entry = jnp.where(all_pos, tab_smem[REL_BUCKETS - 1, h], tab_smem[nb - 1, h])
        return LOG2E * jnp.where(prev_const, entry, 0.0)

    def bias_tile():
        tk, tq = bkt_ref.shape
        row = jnp.broadcast_to(tab_rows[pl.ds(h, 1), :], (tk, LANES)) * LOG2E
        return jnp.concatenate(
            [jnp.take_along_axis(row, bkt_ref[:, c:c + LANES].astype(jnp.int32), axis=1)
             for c in range(0, tq, LANES)], axis=1)

    def finalize():
        lam_vec = lambda r: small_ref[r:r + 1, 0:HEAD_DIM]
        lam = (jnp.exp(jnp.sum(lam_vec(0) * lam_vec(1), axis=-1, keepdims=True))
               - jnp.exp(jnp.sum(lam_vec(2) * lam_vec(3), axis=-1, keepdims=True))
               + LAMBDA_INIT)
        ot = acc1_ref[...] / l1_ref[...] - lam * (acc2_ref[...] / l2_ref[...])
        o = _rms(ot.T, small_ref[4:5, :]) * (1.0 - LAMBDA_INIT)
        o_ref[...] = o.astype(o_ref.dtype)

    cur_kinds = ((cur_const, lambda: None), (jnp.logical_not(cur_const), bias_tile))

    for is_cur, cur_bias in cur_kinds:
        @pl.when(jnp.logical_and(j == 0, is_cur))
        def _(cur_bias=cur_bias):
            m1_ref[...] = jnp.full_like(m1_ref, -jnp.inf)
            m2_ref[...] = jnp.full_like(m2_ref, -jnp.inf)
            l1_ref[...] = jnp.zeros_like(l1_ref)
            l2_ref[...] = jnp.zeros_like(l2_ref)
            acc1_ref[...] = jnp.zeros_like(acc1_ref)
            acc2_ref[...] = jnp.zeros_like(acc2_ref)
            _attn_logits(maps, *slots[0], cur_bias())

    middle = jnp.logical_and(j > 0, j < nk)
    for parity in (0, 1):
        on_parity = jnp.logical_and(middle, j % 2 == parity)
        for is_cur, cur_bias in cur_kinds:
            @pl.when(jnp.logical_and(on_parity, is_cur))
            def _(parity=parity, cur_bias=cur_bias):
                _attn_logits(maps, *slots[parity], cur_bias())
                _attn_accumulate(maps, *slots[1 - parity], vt_ref, bias_scalar())

    @pl.when(j == nk)
    def _():
        _attn_accumulate(maps, *slots[(nk - 1) % 2], vt_ref, bias_scalar())
        finalize()


def _diff_attention(qt, k, vt, positions, rel_table, lq1, lk1, lq2, lk2, subln_g, tq, tk):
    S = k.shape[0]
    H = N_HEADS
    nq, nk = S // tq, S // tk
    pos = positions.reshape(S).astype(jnp.int32)
    qmin = pos.reshape(nq, tq).min(axis=1)
    qmax = pos.reshape(nq, tq).max(axis=1)
    kmin = pos.reshape(nk, tk).min(axis=1)
    kmax = pos.reshape(nk, tk).max(axis=1)
    tab_rows = jnp.zeros((H, LANES), F32).at[:, :REL_BUCKETS].set(rel_table.T)

    buckets = pl.pallas_call(
        _bucket_kernel,
        grid_spec=pltpu.PrefetchScalarGridSpec(
            num_scalar_prefetch=4,
            grid=(nk, nq),
            in_specs=[pl.BlockSpec((1, tq), lambda ki, qi, *_: (0, qi)),
                      pl.BlockSpec((tk, 1), lambda ki, qi, *_: (ki, 0))],
            out_specs=pl.BlockSpec((tk, tq), lambda ki, qi, *_: (ki, qi)),
        ),
        out_shape=jax.ShapeDtypeStruct((S, S), jnp.int8),
        compiler_params=_cparams(("parallel", "arbitrary")),
        name="rel_buckets",
    )(qmin, qmax, kmin, kmax, pos.reshape(1, S), pos.reshape(S, 1))

    per_elem = jnp.logical_not(jnp.logical_or(
        kmin[None, :] - qmax[:, None] >= REL_MAX_DIST,
        kmax[None, :] - qmin[:, None] <= -REL_MAX_DIST))
    tile_of_step = jnp.minimum(jnp.arange(nk + 1, dtype=jnp.int32), nk - 1)
    wanted = jnp.where(per_elem[:, tile_of_step], tile_of_step[None, :], -1)
    held = lax.cummax(wanted, axis=1)
    first = jnp.argmax(per_elem, axis=1).astype(jnp.int32)
    bkt_blk = jnp.where(held >= 0, held, first[:, None]).reshape(-1)

    small = jnp.zeros((SUBLANES, V_DIM), F32)
    small = small.at[0:4, :HEAD_DIM].set(jnp.stack([lq1, lk1, lq2, lk2])).at[4, :].set(subln_g)

    cur = lambda j: jnp.minimum(j, nk - 1)
    prev = lambda j: jnp.maximum(j - 1, 0)
    in_specs = [
        pl.BlockSpec(memory_space=pltpu.SMEM),
        pl.BlockSpec((H, LANES), lambda h, qi, j, *_: (0, 0)),
        pl.BlockSpec((tk, tq),
                     lambda h, qi, j, a, b, c, d, blk: (blk[qi * (nk + 1) + j], qi)),
        pl.BlockSpec((SUBLANES, V_DIM), lambda h, qi, j, *_: (0, 0)),
        pl.BlockSpec((2 * HEAD_DIM, tq), lambda h, qi, j, *_: (h, qi)),
        pl.BlockSpec((tk, 2 * HEAD_DIM), lambda h, qi, j, *_: (cur(j), h)),
        pl.BlockSpec((V_DIM, tk), lambda h, qi, j, *_: (h, prev(j))),
    ]
    state = [pltpu.VMEM((1, tq), F32), pltpu.VMEM((1, tq), F32), pltpu.VMEM((V_DIM, tq), F32)]
    slot = [pltpu.VMEM((tk, tq), F32), pltpu.VMEM((tk, tq), F32),
            pltpu.VMEM((1, tq), F32), pltpu.VMEM((1, tq), F32)]
    grid_spec = pltpu.PrefetchScalarGridSpec(
        num_scalar_prefetch=5,
        grid=(H, nq, nk + 1),
        in_specs=in_specs,
        out_specs=pl.BlockSpec((tq, V_DIM), lambda h, qi, j, *_: (qi, h)),
        scratch_shapes=state + state + slot + slot,
    )
    return pl.pallas_call(
        functools.partial(_attn_kernel, nk=nk),
        grid_spec=grid_spec,
        out_shape=jax.ShapeDtypeStruct((S, H * V_DIM), BF16),
        compiler_params=_cparams(("parallel", "parallel", "arbitrary")),
        name="diff_attn",
    )(qmin, qmax, kmin, kmax, bkt_blk,
      rel_table, tab_rows, buckets, small,
      qt, k, vt)


HALO = 2 * SUBLANES
CONV_COLS_CHUNK = 2 * LANES
CONV_ROWS_CHUNK = 64


def _conv_kernel(prev_ref, cur_ref, next_ref, w_ref, b_ref, g_ref, beta_ref,
                 o_ref, buf_ref, shift_ref, conv_ref):
    i = pl.program_id(0)
    ts, C = cur_ref.shape
    buf_ref[0:HALO, :] = jnp.where(i > 0, prev_ref[...], 0.0)
    buf_ref[HALO:HALO + ts, :] = cur_ref[...]
    buf_ref[HALO + ts:, :] = jnp.where(i < pl.num_programs(0) - 1, next_ref[...], 0.0)
    n_shift = shift_ref.shape[1]

    def col_body(cc, carry):
        c0 = pl.multiple_of(cc * CONV_COLS_CHUNK, CONV_COLS_CHUNK)
        cols = pl.ds(c0, CONV_COLS_CHUNK)
        for r in range(1, SUBLANES):
            shift_ref[r - 1] = buf_ref[r:r + n_shift, cols]
        for r0 in range(0, ts, CONV_ROWS_CHUNK):
            acc = jnp.broadcast_to(b_ref[:, cols], (CONV_ROWS_CHUNK, CONV_COLS_CHUNK))
            for j in range(CONV_WIDTH):
                start = HALO - CONV_PAD + j
                r, base = start % SUBLANES, start - start % SUBLANES + r0
                if r == 0:
                    taps = buf_ref[base:base + CONV_ROWS_CHUNK, cols]
                else:
                    taps = shift_ref[r - 1, base:base + CONV_ROWS_CHUNK, :]
                acc = acc + w_ref[j:j + 1, cols] * taps
            conv_ref[r0:r0 + CONV_ROWS_CHUNK, cols] = acc
        return carry

    lax.fori_loop(0, C // CONV_COLS_CHUNK, col_body, 0)

    c = conv_ref[...]
    mu = jnp.mean(c, axis=-1, keepdims=True)
    d = c - mu
    var = jnp.mean(d * d, axis=-1, keepdims=True)
    y = d * lax.rsqrt(var + EPS) * g_ref[...] + beta_ref[...]
    o_ref[...] = (y * _sigmoid(y)).astype(o_ref.dtype)


def _conv_module(c, w_dw, b_dw, ln_g, ln_b, ts=256):
    S, C = c.shape
    hb = ts // HALO
    last = S // HALO - 1
    vec = pl.BlockSpec((1, C), lambda i: (0, 0))
    return pl.pallas_call(
        _conv_kernel,
        grid=(S // ts,),
        in_specs=[
            pl.BlockSpec((HALO, C), lambda i: (jnp.maximum(i * hb - 1, 0), 0)),
            pl.BlockSpec((ts, C), lambda i: (i, 0)),
            pl.BlockSpec((HALO, C), lambda i: (jnp.minimum((i + 1) * hb, last), 0)),
            pl.BlockSpec((CONV_WIDTH, C), lambda i: (0, 0)),
            vec, vec, vec,
        ],
        out_specs=pl.BlockSpec((ts, C), lambda i: (i, 0)),
        out_shape=jax.ShapeDtypeStruct((S, C), BF16),
        scratch_shapes=[pltpu.VMEM((ts + 2 * HALO, C), F32),
                        pltpu.VMEM((SUBLANES - 1, ts + 2 * HALO - SUBLANES, CONV_COLS_CHUNK), F32),
                        pltpu.VMEM((ts, C), F32)],
        compiler_params=_cparams(("parallel",)),
        name="conv_module",
    )(c, c, c, w_dw.reshape(CONV_WIDTH, C), b_dw.reshape(1, C),
      ln_g.reshape(1, C), ln_b.reshape(1, C))


def kernel(x, p, positions, rel_table, mix_pre_g, w_in, lambda_q1, lambda_k1, lambda_q2, lambda_k2, subln_g, w_attn_o, w_dw, b_dw, conv_ln_g, conv_ln_b, w_conv_o, w_out, mix_post_g, ffn_pre_g, w_up, w_down, ffn_post_g, w_ple_gate, w_ple_proj, ple_post_g):
    B, S, D = x.shape
    assert B == 1 and w_in.shape[0] == 1
    x2d = x.reshape(S, D)
    qkv_cols = 3 * N_HEADS * 2 * HEAD_DIM
    conv_off = qkv_cols
    gate_off = qkv_cols + 2 * D

    p_b = p[0, 0].astype(BF16)

    tall = dict(tm=2048, tn=512, single_buffer_a=True)
    tall_two_weights = dict(tm=2048, tn=256, single_buffer_a=True)

    h = _prenorm(x2d, mix_pre_g[0])
    head_cols = N_HEADS * 2 * HEAD_DIM
    qt = _fused_mm([(h, w_in[0], 0)], [], lambda a: a, head_cols, BF16, name="q_proj",
                   transpose_out=True, **tall)
    k = _fused_mm([(h, w_in[0], head_cols)], [], lambda a: a, head_cols, BF16, name="k_proj",
                  **tall)
    vt = _fused_mm([(h, w_in[0], 2 * head_cols)], [], lambda a: a, head_cols, BF16,
                   name="v_proj", transpose_out=True, **tall)
    glu = _fused_mm([(h, w_in[0], conv_off), (h, w_in[0], conv_off + D)], [],
                    lambda a, b: a * _sigmoid(b), D, F32, name="conv_glu", **tall_two_weights)
    gates = _fused_mm([(h, w_in[0], gate_off)], [], _sigmoid, 2 * D, F32,
                      name="merge_gates", **tall)

    o = _diff_attention(qt, k, vt, positions, rel_table, lambda_q1[0], lambda_k1[0],
                        lambda_q2[0], lambda_k2[0], subln_g[0], tq=512, tk=1024)
    c = _conv_module(glu, w_dw[0], b_dw[0], conv_ln_g[0], conv_ln_b[0])

    merged = _fused_mm([(o, w_attn_o[0], 0), (c, w_conv_o[0], 0)],
                       [(gates, 0), (gates, D)],
                       lambda ya, yc, ga, gc: ga * ya + gc * yc,
                       D, BF16, tm=1024, tn=256, name="branch_merge", single_buffer_a=True)
    mix = _fused_mm([(merged, w_out[0], 0)], [], lambda a: a, D, F32, name="mix_out", **tall)
    x1, h2 = _postnorm_res(x2d, mix, mix_post_g[0], ffn_pre_g[0])

    u = _fused_mm([(h2, w_up[0], 0)], [], lambda a: jnp.square(jnp.maximum(a, 0.0)),
                  w_up.shape[2], BF16, name="ffn_up", **tall)
    dn = _mm_ktiled(u, w_down[0], tm=2048, tn=1024, tk=1024, name="ffn_down")
    x2, x2_b = _postnorm_res(x1, dn, ffn_post_g[0])

    ple = _fused_mm([(x2_b, w_ple_gate[0], 0), (p_b, w_ple_proj[0], 0)], [],
                    lambda a, e: _sigmoid(a) * e, D, F32, name="ple_gate", **tall)
    out = _final_res(x2, ple, ple_post_g[0])
    return out.reshape(B, S, D)
```

```python
import functools
import math

import jax
import jax.numpy as jnp
from jax import lax
from jax.experimental import pallas as pl
from jax.experimental.pallas import tpu as pltpu

N_HEADS = 16
HEAD_DIM = 128
V_DIM = 2 * HEAD_DIM
CONV_WIDTH = 31
CONV_PAD = CONV_WIDTH // 2
REL_BUCKETS = 32
REL_MAX_DIST = 128
EPS = 1e-6
LAMBDA_INIT = 0.8 - 0.6 * math.exp(-0.3 * 0)

LANES = 128
SUBLANES = 8
VMEM_LIMIT_BYTES = 56 * 1024 * 1024

F32 = jnp.float32
BF16 = jnp.bfloat16

MM_TILES = dict(tm=2048, tn=512, single_buffer_a=True)
MM_TILES_TWO_WEIGHTS = dict(tm=2048, tn=256, single_buffer_a=True)
MM_TILES_TWO_LHS = dict(tm=1024, tn=256, single_buffer_a=True)
FFN_DOWN_TILES = dict(tm=2048, tn=1024, tk=1024)
ATTN_TILES = dict(tq=512, tk=1024)


def _cparams(sem):
    return pltpu.CompilerParams(dimension_semantics=sem,
                                vmem_limit_bytes=VMEM_LIMIT_BYTES)


def _sigmoid(x):
    return 1.0 / (1.0 + jnp.exp(-x))


def _rms(x, g):
    return x * lax.rsqrt(jnp.mean(x * x, axis=-1, keepdims=True) + EPS) * g


def _prenorm_kernel(x_ref, g_ref, h_ref):
    h_ref[...] = _rms(x_ref[...], g_ref[...]).astype(h_ref.dtype)


def _prenorm(x, g, tr=256):
    S, D = x.shape
    return pl.pallas_call(
        _prenorm_kernel,
        grid=(S // tr,),
        in_specs=[pl.BlockSpec((tr, D), lambda i: (i, 0)),
                  pl.BlockSpec((1, D), lambda i: (0, 0))],
        out_specs=pl.BlockSpec((tr, D), lambda i: (i, 0)),
        out_shape=jax.ShapeDtypeStruct((S, D), BF16),
        compiler_params=_cparams(("parallel",)),
        name="prenorm",
    )(x, g.reshape(1, D))


def _postnorm_res_kernel(x_ref, y_ref, gpost_ref, *rest, with_prenorm):
    x_new = x_ref[...] + _rms(y_ref[...], gpost_ref[...])
    if with_prenorm:
        gpre_ref, xo_ref, h_ref = rest
        xo_ref[...] = x_new
        h_ref[...] = _rms(x_new, gpre_ref[...]).astype(h_ref.dtype)
    else:
        xo_ref, h_ref = rest
        xo_ref[...] = x_new
        h_ref[...] = x_new.astype(h_ref.dtype)


def _postnorm_res(x, y, g_post, g_pre=None, tr=256):
    S, D = x.shape
    row = pl.BlockSpec((tr, D), lambda i: (i, 0))
    vec = pl.BlockSpec((1, D), lambda i: (0, 0))
    args = [x, y, g_post.reshape(1, D)]
    in_specs = [row, row, vec]
    if g_pre is not None:
        args.append(g_pre.reshape(1, D))
        in_specs.append(vec)
    return pl.pallas_call(
        functools.partial(_postnorm_res_kernel, with_prenorm=g_pre is not None),
        grid=(S // tr,),
        in_specs=in_specs,
        out_specs=[row, row],
        out_shape=[jax.ShapeDtypeStruct((S, D), F32),
                   jax.ShapeDtypeStruct((S, D), BF16)],
        compiler_params=_cparams(("parallel",)),
        name="postnorm_res",
    )(*args)


def _final_kernel(x_ref, y_ref, g_ref, o_ref):
    o_ref[...] = x_ref[...] + _rms(y_ref[...], g_ref[...])


def _final_res(x, y, g, tr=256):
    S, D = x.shape
    row = pl.BlockSpec((tr, D), lambda i: (i, 0))
    return pl.pallas_call(
        _final_kernel,
        grid=(S // tr,),
        in_specs=[row, row, pl.BlockSpec((1, D), lambda i: (0, 0))],
        out_specs=row,
        out_shape=jax.ShapeDtypeStruct((S, D), F32),
        compiler_params=_cparams(("parallel",)),
        name="final_res",
    )(x, y, g.reshape(1, D))


def _fused_mm_kernel(*refs, a_index, epilogue, transpose_out):
    n_a = max(a_index) + 1
    n_pairs = len(a_index)
    out_ref = refs[-1]
    accs = [jnp.dot(refs[a_index[i]][...], refs[n_a + i][...].astype(BF16),
                    preferred_element_type=F32) for i in range(n_pairs)]
    extras = [r[...] for r in refs[n_a + n_pairs:-1]]
    out = epilogue(*accs, *extras)
    out_ref[...] = (out.T if transpose_out else out).astype(out_ref.dtype)


def _fused_mm(pairs, extras, epilogue, n_out, out_dtype, tm, tn, name, single_buffer_a=False,
              transpose_out=False):
    M = pairs[0][0].shape[0]
    a_list, a_index = [], []
    for a, _, _ in pairs:
        for idx, seen in enumerate(a_list):
            if seen is a:
                a_index.append(idx)
                break
        else:
            a_index.append(len(a_list))
            a_list.append(a)
    a_mode = dict(pipeline_mode=pl.Buffered(1)) if single_buffer_a else {}
    in_specs = [pl.BlockSpec((tm, a.shape[1]), lambda m, n: (m, 0), **a_mode) for a in a_list]
    args = list(a_list)
    for a, w, off in pairs:
        ob = off // tn
        in_specs.append(pl.BlockSpec((a.shape[1], tn), lambda m, n, ob=ob: (0, n + ob)))
        args.append(w)
    for e, off in extras:
        ob = off // tn
        in_specs.append(pl.BlockSpec((tm, tn), lambda m, n, ob=ob: (m, n + ob)))
        args.append(e)
    return pl.pallas_call(
        functools.partial(_fused_mm_kernel, a_index=tuple(a_index), epilogue=epilogue,
                          transpose_out=transpose_out),
        grid=(M // tm, n_out // tn),
        in_specs=in_specs,
        out_specs=(pl.BlockSpec((tn, tm), lambda m, n: (n, m)) if transpose_out
                   else pl.BlockSpec((tm, tn), lambda m, n: (m, n))),
        out_shape=jax.ShapeDtypeStruct((n_out, M) if transpose_out else (M, n_out), out_dtype),
        compiler_params=_cparams(("parallel", "arbitrary")),
        name=name,
    )(*args)


def _mm_ktiled_kernel(a_ref, w_ref, o_ref):
    k = pl.program_id(2)

    def part():
        return jnp.dot(a_ref[...], w_ref[...].astype(BF16), preferred_element_type=F32)

    @pl.when(k == 0)
    def _():
        o_ref[...] = part()

    @pl.when(k > 0)
    def _():
        o_ref[...] += part()


def _mm_ktiled(a, w, tm, tn, tk, name):
    M, K = a.shape
    N = w.shape[1]
    return pl.pallas_call(
        _mm_ktiled_kernel,
        grid=(M // tm, N // tn, K // tk),
        in_specs=[pl.BlockSpec((tm, tk), lambda m, n, k: (m, k)),
                  pl.BlockSpec((tk, tn), lambda m, n, k: (k, n))],
        out_specs=pl.BlockSpec((tm, tn), lambda m, n, k: (m, n)),
        out_shape=jax.ShapeDtypeStruct((M, N), F32),
        compiler_params=_cparams(("parallel", "parallel", "arbitrary")),
        name=name,
    )(a, w)


def _rel_bucket(rel):
    nb = REL_BUCKETS // 2
    max_exact = nb // 2
    ret = jnp.where(rel > 0, nb, 0)
    n = jnp.abs(rel)
    nf = jnp.maximum(n, 1).astype(F32)
    large = max_exact + (jnp.log(nf / max_exact) / math.log(REL_MAX_DIST / max_exact)
                         * (nb - max_exact)).astype(jnp.int32)
    large = jnp.minimum(large, nb - 1)
    return ret + jnp.where(n < max_exact, n, large)


LOG2E = math.log2(math.e)
LOGIT_SCALE_LOG2 = (HEAD_DIM ** -0.5) * LOG2E
ATTN_KEYS_CHUNK = 512
SMALL_LAMBDA_ROWS = (0, 1, 2, 3)
SMALL_GAIN_ROW = 4


def _attn_logits(maps, s_scrs, r_scrs, bias_l=None):
    logits = [jnp.dot(k_ref[...], qt_ref[...], preferred_element_type=F32)
              for qt_ref, k_ref, _, _, _ in maps]
    for s, s_scr, r_scr in zip(logits, s_scrs, r_scrs):
        s = s * LOGIT_SCALE_LOG2
        if bias_l is not None:
            s = s + bias_l
        s_scr[...] = s
        r_scr[...] = jnp.max(s, axis=0, keepdims=True)


def _attn_accumulate(maps, s_scrs, r_scrs, vt_ref, bias_l):
    tk = vt_ref.shape[1]
    rows = []
    for (_, _, m_ref, _, _), r_scr in zip(maps, r_scrs):
        m_prev = m_ref[...]
        m_new = jnp.maximum(m_prev, r_scr[...] + bias_l)
        shift = bias_l - m_new
        m_ref[...] = m_new
        rows.append((jnp.exp2(m_prev - m_new), shift))
    sums = [None, None]
    probs = [[], []]
    for c in range(0, tk, ATTN_KEYS_CHUNK):
        for i, ((_, shift), s_scr) in enumerate(zip(rows, s_scrs)):
            s = s_scr[c:c + ATTN_KEYS_CHUNK, :]
            p = jnp.exp2(s + shift)
            psum = jnp.sum(p, axis=0, keepdims=True)
            sums[i] = psum if sums[i] is None else sums[i] + psum
            probs[i].append(p.astype(vt_ref.dtype))
    for (alpha, _), psum, blocks, (_, _, _, l_ref, acc_ref) in zip(rows, sums, probs, maps):
        l_ref[...] = alpha * l_ref[...] + psum
        acc_ref[...] = alpha * acc_ref[...] + jnp.dot(
            vt_ref[...], jnp.concatenate(blocks, axis=0), preferred_element_type=F32)


def _bucket_kernel(qmin_ref, qmax_ref, kmin_ref, kmax_ref, posq_ref, posk_ref, o_ref):
    ki = pl.program_id(0)
    qi = pl.program_id(1)
    const_bias = _tile_bias_is_const(qmin_ref, qmax_ref, kmin_ref, kmax_ref, qi, ki)[0]

    @pl.when(const_bias)
    def _():
        o_ref[...] = jnp.zeros_like(o_ref)

    @pl.when(jnp.logical_not(const_bias))
    def _():
        o_ref[...] = _rel_bucket(posk_ref[...] - posq_ref[...]).astype(o_ref.dtype)


def _tile_bias_is_const(qmin_ref, qmax_ref, kmin_ref, kmax_ref, qi, ki):
    all_pos = kmin_ref[ki] - qmax_ref[qi] >= REL_MAX_DIST
    all_neg = kmax_ref[ki] - qmin_ref[qi] <= -REL_MAX_DIST
    return jnp.logical_or(all_pos, all_neg), all_pos


def _attn_kernel(qmin_ref, qmax_ref, kmin_ref, kmax_ref, bkt_blk_ref,
                 tab_smem, tab_rows, bkt_ref, small_ref,
                 qt_ref, k_ref, vt_ref,
                 o_ref,
                 m1_ref, l1_ref, acc1_ref, m2_ref, l2_ref, acc2_ref,
                 s1a_scr, s2a_scr, r1a_scr, r2a_scr,
                 s1b_scr, s2b_scr, r1b_scr, r2b_scr, *, nk):
    h = pl.program_id(0)
    qi = pl.program_id(1)
    j = pl.program_id(2)
    lo, hi = pl.ds(0, HEAD_DIM), pl.ds(HEAD_DIM, HEAD_DIM)
    maps = [(qt_ref.at[lo, :], k_ref.at[:, lo], m1_ref, l1_ref, acc1_ref),
            (qt_ref.at[hi, :], k_ref.at[:, hi], m2_ref, l2_ref, acc2_ref)]
    slots = [((s1a_scr, s2a_scr), (r1a_scr, r2a_scr)),
             ((s1b_scr, s2b_scr), (r1b_scr, r2b_scr))]

    prev_const, all_pos = _tile_bias_is_const(qmin_ref, qmax_ref, kmin_ref, kmax_ref,
                                              qi, jnp.maximum(j - 1, 0))
    cur_const = _tile_bias_is_const(qmin_ref, qmax_ref, kmin_ref, kmax_ref,
                                    qi, jnp.minimum(j, nk - 1))[0]
    nb = REL_BUCKETS // 2

    def bias_scalar():
        entry = jnp.where(all_pos, tab_smem[REL_BUCKETS - 1, h], tab_smem[nb - 1, h])
        return LOG2E * jnp.where(prev_const, entry, 0.0)

    def bias_tile():
        tk, tq = bkt_ref.shape
        row = jnp.broadcast_to(tab_rows[pl.ds(h, 1), :], (tk, LANES)) * LOG2E
        return jnp.concatenate(
            [jnp.take_along_axis(row, bkt_ref[:, c:c + LANES].astype(jnp.int32), axis=1)
             for c in range(0, tq, LANES)], axis=1)

    def finalize():
        lq1, lk1, lq2, lk2 = (small_ref[r:r + 1, 0:HEAD_DIM] for r in SMALL_LAMBDA_ROWS)
        lam = (jnp.exp(jnp.sum(lq1 * lk1, axis=-1, keepdims=True))
               - jnp.exp(jnp.sum(lq2 * lk2, axis=-1, keepdims=True))
               + LAMBDA_INIT)
        ot = acc1_ref[...] / l1_ref[...] - lam * (acc2_ref[...] / l2_ref[...])
        gain = small_ref[SMALL_GAIN_ROW:SMALL_GAIN_ROW + 1, :]
        o = _rms(ot.T, gain) * (1.0 - LAMBDA_INIT)
        o_ref[...] = o.astype(o_ref.dtype)

    cur_kinds = ((cur_const, lambda: None), (jnp.logical_not(cur_const), bias_tile))

    for is_cur, cur_bias in cur_kinds:
        @pl.when(jnp.logical_and(j == 0, is_cur))
        def _(cur_bias=cur_bias):
            m1_ref[...] = jnp.full_like(m1_ref, -jnp.inf)
            m2_ref[...] = jnp.full_like(m2_ref, -jnp.inf)
            l1_ref[...] = jnp.zeros_like(l1_ref)
            l2_ref[...] = jnp.zeros_like(l2_ref)
            acc1_ref[...] = jnp.zeros_like(acc1_ref)
            acc2_ref[...] = jnp.zeros_like(acc2_ref)
            _attn_logits(maps, *slots[0], cur_bias())

    middle = jnp.logical_and(j > 0, j < nk)
    for parity in (0, 1):
        on_parity = jnp.logical_and(middle, j % 2 == parity)
        for is_cur, cur_bias in cur_kinds:
            @pl.when(jnp.logical_and(on_parity, is_cur))
            def _(parity=parity, cur_bias=cur_bias):
                _attn_logits(maps, *slots[parity], cur_bias())
                _attn_accumulate(maps, *slots[1 - parity], vt_ref, bias_scalar())

    @pl.when(j == nk)
    def _():
        _attn_accumulate(maps, *slots[(nk - 1) % 2], vt_ref, bias_scalar())
        finalize()


def _diff_attention(qt, k, vt, positions, rel_table, lq1, lk1, lq2, lk2, subln_g, tq, tk):
    S = k.shape[0]
    H = N_HEADS
    nq, nk = S // tq, S // tk
    pos = positions.reshape(S).astype(jnp.int32)
    qmin = pos.reshape(nq, tq).min(axis=1)
    qmax = pos.reshape(nq, tq).max(axis=1)
    kmin = pos.reshape(nk, tk).min(axis=1)
    kmax = pos.reshape(nk, tk).max(axis=1)
    tab_rows = jnp.zeros((H, LANES), F32).at[:, :REL_BUCKETS].set(rel_table.T)

    buckets = pl.pallas_call(
        _bucket_kernel,
        grid_spec=pltpu.PrefetchScalarGridSpec(
            num_scalar_prefetch=4,
            grid=(nk, nq),
            in_specs=[pl.BlockSpec((1, tq), lambda ki, qi, *_: (0, qi)),
                      pl.BlockSpec((tk, 1), lambda ki, qi, *_: (ki, 0))],
            out_specs=pl.BlockSpec((tk, tq), lambda ki, qi, *_: (ki, qi)),
        ),
        out_shape=jax.ShapeDtypeStruct((S, S), jnp.int8),
        compiler_params=_cparams(("parallel", "arbitrary")),
        name="rel_buckets",
    )(qmin, qmax, kmin, kmax, pos.reshape(1, S), pos.reshape(S, 1))

    per_elem = jnp.logical_not(jnp.logical_or(
        kmin[None, :] - qmax[:, None] >= REL_MAX_DIST,
        kmax[None, :] - qmin[:, None] <= -REL_MAX_DIST))
    tile_of_step = jnp.minimum(jnp.arange(nk + 1, dtype=jnp.int32), nk - 1)
    wanted = jnp.where(per_elem[:, tile_of_step], tile_of_step[None, :], -1)
    held = lax.cummax(wanted, axis=1)
    first = jnp.argmax(per_elem, axis=1).astype(jnp.int32)
    bkt_blk = jnp.where(held >= 0, held, first[:, None]).reshape(-1)

    small = jnp.zeros((SUBLANES, V_DIM), F32)
    for row, vec in zip(SMALL_LAMBDA_ROWS, (lq1, lk1, lq2, lk2)):
        small = small.at[row, :HEAD_DIM].set(vec)
    small = small.at[SMALL_GAIN_ROW, :].set(subln_g)

    cur = lambda j: jnp.minimum(j, nk - 1)
    prev = lambda j: jnp.maximum(j - 1, 0)
    in_specs = [
        pl.BlockSpec(memory_space=pltpu.SMEM),
        pl.BlockSpec((H, LANES), lambda h, qi, j, *_: (0, 0)),
        pl.BlockSpec((tk, tq),
                     lambda h, qi, j, a, b, c, d, blk: (blk[qi * (nk + 1) + j], qi)),
        pl.BlockSpec((SUBLANES, V_DIM), lambda h, qi, j, *_: (0, 0)),
        pl.BlockSpec((2 * HEAD_DIM, tq), lambda h, qi, j, *_: (h, qi)),
        pl.BlockSpec((tk, 2 * HEAD_DIM), lambda h, qi, j, *_: (cur(j), h)),
        pl.BlockSpec((V_DIM, tk), lambda h, qi, j, *_: (h, prev(j))),
    ]
    state = [pltpu.VMEM((1, tq), F32), pltpu.VMEM((1, tq), F32), pltpu.VMEM((V_DIM, tq), F32)]
    slot = [pltpu.VMEM((tk, tq), F32), pltpu.VMEM((tk, tq), F32),
            pltpu.VMEM((1, tq), F32), pltpu.VMEM((1, tq), F32)]
    grid_spec = pltpu.PrefetchScalarGridSpec(
        num_scalar_prefetch=5,
        grid=(H, nq, nk + 1),
        in_specs=in_specs,
        out_specs=pl.BlockSpec((tq, V_DIM), lambda h, qi, j, *_: (qi, h)),
        scratch_shapes=state + state + slot + slot,
    )
    return pl.pallas_call(
        functools.partial(_attn_kernel, nk=nk),
        grid_spec=grid_spec,
        out_shape=jax.ShapeDtypeStruct((S, H * V_DIM), BF16),
        compiler_params=_cparams(("parallel", "parallel", "arbitrary")),
        name="diff_attn",
    )(qmin, qmax, kmin, kmax, bkt_blk,
      rel_table, tab_rows, buckets, small,
      qt, k, vt)


HALO = 2 * SUBLANES
CONV_COLS_CHUNK = 2 * LANES
CONV_ROWS_CHUNK = 64


def _conv_kernel(prev_ref, cur_ref, next_ref, w_ref, b_ref, g_ref, beta_ref,
                 o_ref, buf_ref, shift_ref, conv_ref):
    i = pl.program_id(0)
    ts, C = cur_ref.shape
    buf_ref[0:HALO, :] = jnp.where(i > 0, prev_ref[...], 0.0)
    buf_ref[HALO:HALO + ts, :] = cur_ref[...]
    buf_ref[HALO + ts:, :] = jnp.where(i < pl.num_programs(0) - 1, next_ref[...], 0.0)
    n_shift = shift_ref.shape[1]

    def col_body(cc, carry):
        c0 = pl.multiple_of(cc * CONV_COLS_CHUNK, CONV_COLS_CHUNK)
        cols = pl.ds(c0, CONV_COLS_CHUNK)
        for r in range(1, SUBLANES):
            shift_ref[r - 1] = buf_ref[r:r + n_shift, cols]
        for r0 in range(0, ts, CONV_ROWS_CHUNK):
            acc = jnp.broadcast_to(b_ref[:, cols], (CONV_ROWS_CHUNK, CONV_COLS_CHUNK))
            for j in range(CONV_WIDTH):
                start = HALO - CONV_PAD + j
                r, base = start % SUBLANES, start - start % SUBLANES + r0
                if r == 0:
                    taps = buf_ref[base:base + CONV_ROWS_CHUNK, cols]
                else:
                    taps = shift_ref[r - 1, base:base + CONV_ROWS_CHUNK, :]
                acc = acc + w_ref[j:j + 1, cols] * taps
            conv_ref[r0:r0 + CONV_ROWS_CHUNK, cols] = acc
        return carry

    lax.fori_loop(0, C // CONV_COLS_CHUNK, col_body, 0)

    c = conv_ref[...]
    mu = jnp.mean(c, axis=-1, keepdims=True)
    d = c - mu
    var = jnp.mean(d * d, axis=-1, keepdims=True)
    y = d * lax.rsqrt(var + EPS) * g_ref[...] + beta_ref[...]
    o_ref[...] = (y * _sigmoid(y)).astype(o_ref.dtype)


def _conv_module(c, w_dw, b_dw, ln_g, ln_b, ts=256):
    S, C = c.shape
    hb = ts // HALO
    last = S // HALO - 1
    vec = pl.BlockSpec((1, C), lambda i: (0, 0))
    return pl.pallas_call(
        _conv_kernel,
        grid=(S // ts,),
        in_specs=[
            pl.BlockSpec((HALO, C), lambda i: (jnp.maximum(i * hb - 1, 0), 0)),
            pl.BlockSpec((ts, C), lambda i: (i, 0)),
            pl.BlockSpec((HALO, C), lambda i: (jnp.minimum((i + 1) * hb, last), 0)),
            pl.BlockSpec((CONV_WIDTH, C), lambda i: (0, 0)),
            vec, vec, vec,
        ],
        out_specs=pl.BlockSpec((ts, C), lambda i: (i, 0)),
        out_shape=jax.ShapeDtypeStruct((S, C), BF16),
        scratch_shapes=[pltpu.VMEM((ts + 2 * HALO, C), F32),
                        pltpu.VMEM((SUBLANES - 1, ts + 2 * HALO - SUBLANES, CONV_COLS_CHUNK), F32),
                        pltpu.VMEM((ts, C), F32)],
        compiler_params=_cparams(("parallel",)),
        name="conv_module",
    )(c, c, c, w_dw.reshape(CONV_WIDTH, C), b_dw.reshape(1, C),
      ln_g.reshape(1, C), ln_b.reshape(1, C))


def kernel(x, p, positions, rel_table, mix_pre_g, w_in, lambda_q1, lambda_k1, lambda_q2, lambda_k2, subln_g, w_attn_o, w_dw, b_dw, conv_ln_g, conv_ln_b, w_conv_o, w_out, mix_post_g, ffn_pre_g, w_up, w_down, ffn_post_g, w_ple_gate, w_ple_proj, ple_post_g):
    B, S, D = x.shape
    assert B == 1 and w_in.shape[0] == 1
    x2d = x.reshape(S, D)
    qkv_cols = 3 * N_HEADS * 2 * HEAD_DIM
    conv_off = qkv_cols
    gate_off = qkv_cols + 2 * D

    p_b = p[0, 0].astype(BF16)

    h = _prenorm(x2d, mix_pre_g[0])
    head_cols = N_HEADS * 2 * HEAD_DIM
    qt = _fused_mm([(h, w_in[0], 0)], [], lambda a: a, head_cols, BF16, name="q_proj",
                   transpose_out=True, **MM_TILES)
    k = _fused_mm([(h, w_in[0], head_cols)], [], lambda a: a, head_cols, BF16, name="k_proj",
                  **MM_TILES)
    vt = _fused_mm([(h, w_in[0], 2 * head_cols)], [], lambda a: a, head_cols, BF16,
                   name="v_proj", transpose_out=True, **MM_TILES)
    glu = _fused_mm([(h, w_in[0], conv_off), (h, w_in[0], conv_off + D)], [],
                    lambda a, b: a * _sigmoid(b), D, F32, name="conv_glu",
                    **MM_TILES_TWO_WEIGHTS)
    gates = _fused_mm([(h, w_in[0], gate_off)], [], _sigmoid, 2 * D, F32,
                      name="merge_gates", **MM_TILES)

    o = _diff_attention(qt, k, vt, positions, rel_table, lambda_q1[0], lambda_k1[0],
                        lambda_q2[0], lambda_k2[0], subln_g[0], **ATTN_TILES)
    c = _conv_module(glu, w_dw[0], b_dw[0], conv_ln_g[0], conv_ln_b[0])

    merged = _fused_mm([(o, w_attn_o[0], 0), (c, w_conv_o[0], 0)],
                       [(gates, 0), (gates, D)],
                       lambda ya, yc, ga, gc: ga * ya + gc * yc,
                       D, BF16, name="branch_merge", **MM_TILES_TWO_LHS)
    mix = _fused_mm([(merged, w_out[0], 0)], [], lambda a: a, D, F32, name="mix_out",
                    **MM_TILES)
    x1, h2 = _postnorm_res(x2d, mix, mix_post_g[0], ffn_pre_g[0])

    u = _fused_mm([(h2, w_up[0], 0)], [], lambda a: jnp.square(jnp.maximum(a, 0.0)),
                  w_up.shape[2], BF16, name="ffn_up", **MM_TILES)
    dn = _mm_ktiled(u, w_down[0], name="ffn_down", **FFN_DOWN_TILES)
    x2, x2_b = _postnorm_res(x1, dn, ffn_post_g[0])

    ple = _fused_mm([(x2_b, w_ple_gate[0], 0), (p_b, w_ple_proj[0], 0)], [],
                    lambda a, e: _sigmoid(a) * e, D, F32, name="ple_gate", **MM_TILES)
    out = _final_res(x2, ple, ple_post_g[0])
    return out.reshape(B, S, D)
```

```python
import functools
import math

import jax
import jax.numpy as jnp
from jax import lax
from jax.experimental import pallas as pl
from jax.experimental.pallas import tpu as pltpu

N_HEADS = 16
HEAD_DIM = 128
V_DIM = 2 * HEAD_DIM
CONV_WIDTH = 31
CONV_PAD = CONV_WIDTH // 2
REL_BUCKETS = 32
REL_MAX_DIST = 128
EPS = 1e-6
LAMBDA_INIT = 0.8 - 0.6 * math.exp(-0.3 * 0)

LANES = 128
SUBLANES = 8
VMEM_LIMIT_BYTES = 56 * 1024 * 1024

F32 = jnp.float32
BF16 = jnp.bfloat16

MM_TILES = dict(tm=2048, tn=512, single_buffer_a=True)
MM_TILES_TWO_WEIGHTS = dict(tm=2048, tn=256, single_buffer_a=True)
MM_TILES_TWO_LHS = dict(tm=1024, tn=256, single_buffer_a=True)
FFN_DOWN_TILES = dict(tm=2048, tn=1024, tk=1024)
ATTN_TILES = dict(tq=512, tk=1024)


def _cparams(sem):
    return pltpu.CompilerParams(dimension_semantics=sem,
                                vmem_limit_bytes=VMEM_LIMIT_BYTES)


def _sigmoid(x):
    return 0.5 * jnp.tanh(0.5 * x) + 0.5


def _rms(x, g):
    return x * lax.rsqrt(jnp.mean(x * x, axis=-1, keepdims=True) + EPS) * g


def _prenorm_kernel(x_ref, g_ref, h_ref):
    h_ref[...] = _rms(x_ref[...], g_ref[...]).astype(h_ref.dtype)


def _prenorm(x, g, tr=256):
    S, D = x.shape
    return pl.pallas_call(
        _prenorm_kernel,
        grid=(S // tr,),
        in_specs=[pl.BlockSpec((tr, D), lambda i: (i, 0)),
                  pl.BlockSpec((1, D), lambda i: (0, 0))],
        out_specs=pl.BlockSpec((tr, D), lambda i: (i, 0)),
        out_shape=jax.ShapeDtypeStruct((S, D), BF16),
        compiler_params=_cparams(("parallel",)),
        name="prenorm",
    )(x, g.reshape(1, D))


def _postnorm_res_kernel(x_ref, y_ref, gpost_ref, *rest, with_prenorm):
    x_new = x_ref[...] + _rms(y_ref[...], gpost_ref[...])
    if with_prenorm:
        gpre_ref, xo_ref, h_ref = rest
        xo_ref[...] = x_new
        h_ref[...] = _rms(x_new, gpre_ref[...]).astype(h_ref.dtype)
    else:
        xo_ref, h_ref = rest
        xo_ref[...] = x_new
        h_ref[...] = x_new.astype(h_ref.dtype)


def _postnorm_res(x, y, g_post, g_pre=None, tr=256):
    S, D = x.shape
    row = pl.BlockSpec((tr, D), lambda i: (i, 0))
    vec = pl.BlockSpec((1, D), lambda i: (0, 0))
    args = [x, y, g_post.reshape(1, D)]
    in_specs = [row, row, vec]
    if g_pre is not None:
        args.append(g_pre.reshape(1, D))
        in_specs.append(vec)
    return pl.pallas_call(
        functools.partial(_postnorm_res_kernel, with_prenorm=g_pre is not None),
        grid=(S // tr,),
        in_specs=in_specs,
        out_specs=[row, row],
        out_shape=[jax.ShapeDtypeStruct((S, D), F32),
                   jax.ShapeDtypeStruct((S, D), BF16)],
        compiler_params=_cparams(("parallel",)),
        name="postnorm_res",
    )(*args)


def _final_kernel(x_ref, y_ref, g_ref, o_ref):
    o_ref[...] = x_ref[...] + _rms(y_ref[...], g_ref[...])


def _final_res(x, y, g, tr=256):
    S, D = x.shape
    row = pl.BlockSpec((tr, D), lambda i: (i, 0))
    return pl.pallas_call(
        _final_kernel,
        grid=(S // tr,),
        in_specs=[row, row, pl.BlockSpec((1, D), lambda i: (0, 0))],
        out_specs=row,
        out_shape=jax.ShapeDtypeStruct((S, D), F32),
        compiler_params=_cparams(("parallel",)),
        name="final_res",
    )(x, y, g.reshape(1, D))


def _fused_mm_kernel(*refs, a_index, epilogue, transpose_out):
    n_a = max(a_index) + 1
    n_pairs = len(a_index)
    out_ref = refs[-1]
    accs = [jnp.dot(refs[a_index[i]][...], refs[n_a + i][...].astype(BF16),
                    preferred_element_type=F32) for i in range(n_pairs)]
    extras = [r[...] for r in refs[n_a + n_pairs:-1]]
    out = epilogue(*accs, *extras)
    out_ref[...] = (out.T if transpose_out else out).astype(out_ref.dtype)


def _fused_mm(pairs, extras, epilogue, n_out, out_dtype, tm, tn, name, single_buffer_a=False,
              transpose_out=False):
    M = pairs[0][0].shape[0]
    a_list, a_index = [], []
    for a, _, _ in pairs:
        for idx, seen in enumerate(a_list):
            if seen is a:
                a_index.append(idx)
                break
        else:
            a_index.append(len(a_list))
            a_list.append(a)
    a_mode = dict(pipeline_mode=pl.Buffered(1)) if single_buffer_a else {}
    in_specs = [pl.BlockSpec((tm, a.shape[1]), lambda m, n: (m, 0), **a_mode) for a in a_list]
    args = list(a_list)
    for a, w, off in pairs:
        ob = off // tn
        in_specs.append(pl.BlockSpec((a.shape[1], tn), lambda m, n, ob=ob: (0, n + ob)))
        args.append(w)
    for e, off in extras:
        ob = off // tn
        in_specs.append(pl.BlockSpec((tm, tn), lambda m, n, ob=ob: (m, n + ob)))
        args.append(e)
    return pl.pallas_call(
        functools.partial(_fused_mm_kernel, a_index=tuple(a_index), epilogue=epilogue,
                          transpose_out=transpose_out),
        grid=(M // tm, n_out // tn),
        in_specs=in_specs,
        out_specs=(pl.BlockSpec((tn, tm), lambda m, n: (n, m)) if transpose_out
                   else pl.BlockSpec((tm, tn), lambda m, n: (m, n))),
        out_shape=jax.ShapeDtypeStruct((n_out, M) if transpose_out else (M, n_out), out_dtype),
        compiler_params=_cparams(("parallel", "arbitrary")),
        name=name,
    )(*args)


def _mm_ktiled_kernel(a_ref, w_ref, o_ref):
    k = pl.program_id(2)

    def part():
        return jnp.dot(a_ref[...], w_ref[...].astype(BF16), preferred_element_type=F32)

    @pl.when(k == 0)
    def _():
        o_ref[...] = part()

    @pl.when(k > 0)
    def _():
        o_ref[...] += part()


def _mm_ktiled(a, w, tm, tn, tk, name):
    M, K = a.shape
    N = w.shape[1]
    return pl.pallas_call(
        _mm_ktiled_kernel,
        grid=(M // tm, N // tn, K // tk),
        in_specs=[pl.BlockSpec((tm, tk), lambda m, n, k: (m, k)),
                  pl.BlockSpec((tk, tn), lambda m, n, k: (k, n))],
        out_specs=pl.BlockSpec((tm, tn), lambda m, n, k: (m, n)),
        out_shape=jax.ShapeDtypeStruct((M, N), F32),
        compiler_params=_cparams(("parallel", "parallel", "arbitrary")),
        name=name,
    )(a, w)


def _rel_bucket(rel):
    nb = REL_BUCKETS // 2
    max_exact = nb // 2
    ret = jnp.where(rel > 0, nb, 0)
    n = jnp.abs(rel)
    nf = jnp.maximum(n, 1).astype(F32)
    large = max_exact + (jnp.log(nf / max_exact) / math.log(REL_MAX_DIST / max_exact)
                         * (nb - max_exact)).astype(jnp.int32)
    large = jnp.minimum(large, nb - 1)
    return ret + jnp.where(n < max_exact, n, large)


LOG2E = math.log2(math.e)
LOGIT_SCALE_LOG2 = (HEAD_DIM ** -0.5) * LOG2E
ATTN_KEYS_CHUNK = 512
SMALL_LAMBDA_ROWS = (0, 1, 2, 3)
SMALL_GAIN_ROW = 4


def _attn_logits(maps, s_scrs, r_scrs, bias_l=None):
    logits = [jnp.dot(k_ref[...], qt_ref[...], preferred_element_type=F32)
              for qt_ref, k_ref, _, _, _ in maps]
    for s, s_scr, r_scr in zip(logits, s_scrs, r_scrs):
        s = s * LOGIT_SCALE_LOG2
        if bias_l is not None:
            s = s + bias_l
        s_scr[...] = s
        r_scr[...] = jnp.max(s, axis=0, keepdims=True)


def _attn_accumulate(maps, s_scrs, r_scrs, vt_ref, bias_l):
    tk = vt_ref.shape[1]
    rows = []
    for (_, _, m_ref, _, _), r_scr in zip(maps, r_scrs):
        m_prev = m_ref[...]
        m_new = jnp.maximum(m_prev, r_scr[...] + bias_l)
        shift = bias_l - m_new
        m_ref[...] = m_new
        rows.append((jnp.exp2(m_prev - m_new), shift))
    sums = [None, None]
    probs = [[], []]
    for c in range(0, tk, ATTN_KEYS_CHUNK):
        for i, ((_, shift), s_scr) in enumerate(zip(rows, s_scrs)):
            s = s_scr[c:c + ATTN_KEYS_CHUNK, :]
            p = jnp.exp2(s + shift)
            psum = jnp.sum(p, axis=0, keepdims=True)
            sums[i] = psum if sums[i] is None else sums[i] + psum
            probs[i].append(p.astype(vt_ref.dtype))
    for (alpha, _), psum, blocks, (_, _, _, l_ref, acc_ref) in zip(rows, sums, probs, maps):
        l_ref[...] = alpha * l_ref[...] + psum
        acc_ref[...] = alpha * acc_ref[...] + jnp.dot(
            vt_ref[...], jnp.concatenate(blocks, axis=0), preferred_element_type=F32)


def _bucket_kernel(qmin_ref, qmax_ref, kmin_ref, kmax_ref, posq_ref, posk_ref, o_ref):
    ki = pl.program_id(0)
    qi = pl.program_id(1)
    const_bias = _tile_bias_is_const(qmin_ref, qmax_ref, kmin_ref, kmax_ref, qi, ki)[0]

    @pl.when(const_bias)
    def _():
        o_ref[...] = jnp.zeros_like(o_ref)

    @pl.when(jnp.logical_not(const_bias))
    def _():
        o_ref[...] = _rel_bucket(posk_ref[...] - posq_ref[...]).astype(o_ref.dtype)


def _tile_bias_is_const(qmin_ref, qmax_ref, kmin_ref, kmax_ref, qi, ki):
    all_pos = kmin_ref[ki] - qmax_ref[qi] >= REL_MAX_DIST
    all_neg = kmax_ref[ki] - qmin_ref[qi] <= -REL_MAX_DIST
    return jnp.logical_or(all_pos, all_neg), all_pos


def _attn_kernel(qmin_ref, qmax_ref, kmin_ref, kmax_ref, bkt_blk_ref,
                 tab_smem, tab_rows, bkt_ref, small_ref,
                 qt_ref, k_ref, vt_ref,
                 o_ref,
                 m1_ref, l1_ref, acc1_ref, m2_ref, l2_ref, acc2_ref,
                 s1a_scr, s2a_scr, r1a_scr, r2a_scr,
                 s1b_scr, s2b_scr, r1b_scr, r2b_scr, *, nk):
    h = pl.program_id(0)
    qi = pl.program_id(1)
    j = pl.program_id(2)
    lo, hi = pl.ds(0, HEAD_DIM), pl.ds(HEAD_DIM, HEAD_DIM)
    maps = [(qt_ref.at[lo, :], k_ref.at[:, lo], m1_ref, l1_ref, acc1_ref),
            (qt_ref.at[hi, :], k_ref.at[:, hi], m2_ref, l2_ref, acc2_ref)]
    slots = [((s1a_scr, s2a_scr), (r1a_scr, r2a_scr)),
             ((s1b_scr, s2b_scr), (r1b_scr, r2b_scr))]

    prev_const, all_pos = _tile_bias_is_const(qmin_ref, qmax_ref, kmin_ref, kmax_ref,
                                              qi, jnp.maximum(j - 1, 0))
    cur_const = _tile_bias_is_const(qmin_ref, qmax_ref, kmin_ref, kmax_ref,
                                    qi, jnp.minimum(j, nk - 1))[0]
    nb = REL_BUCKETS // 2

    def bias_scalar():
        entry = jnp.where(all_pos, tab_smem[REL_BUCKETS - 1, h], tab_smem[nb - 1, h])
        return LOG2E * jnp.where(prev_const, entry, 0.0)

    def bias_tile():
        tk, tq = bkt_ref.shape
        row = jnp.broadcast_to(tab_rows[pl.ds(h, 1), :], (tk, LANES)) * LOG2E
        return jnp.concatenate(
            [jnp.take_along_axis(row, bkt_ref[:, c:c + LANES].astype(jnp.int32), axis=1)
             for c in range(0, tq, LANES)], axis=1)

    def finalize():
        lq1, lk1, lq2, lk2 = (small_ref[r:r + 1, 0:HEAD_DIM] for r in SMALL_LAMBDA_ROWS)
        lam = (jnp.exp(jnp.sum(lq1 * lk1, axis=-1, keepdims=True))
               - jnp.exp(jnp.sum(lq2 * lk2, axis=-1, keepdims=True))
               + LAMBDA_INIT)
        ot = acc1_ref[...] / l1_ref[...] - lam * (acc2_ref[...] / l2_ref[...])
        gain = small_ref[SMALL_GAIN_ROW:SMALL_GAIN_ROW + 1, :]
        o = _rms(ot.T, gain) * (1.0 - LAMBDA_INIT)
        o_ref[...] = o.astype(o_ref.dtype)

    cur_kinds = ((cur_const, lambda: None), (jnp.logical_not(cur_const), bias_tile))

    for is_cur, cur_bias in cur_kinds:
        @pl.when(jnp.logical_and(j == 0, is_cur))
        def _(cur_bias=cur_bias):
            m1_ref[...] = jnp.full_like(m1_ref, -jnp.inf)
            m2_ref[...] = jnp.full_like(m2_ref, -jnp.inf)
            l1_ref[...] = jnp.zeros_like(l1_ref)
            l2_ref[...] = jnp.zeros_like(l2_ref)
            acc1_ref[...] = jnp.zeros_like(acc1_ref)
            acc2_ref[...] = jnp.zeros_like(acc2_ref)
            _attn_logits(maps, *slots[0], cur_bias())

    middle = jnp.logical_and(j > 0, j < nk)
    for parity in (0, 1):
        on_parity = jnp.logical_and(middle, j % 2 == parity)
        for is_cur, cur_bias in cur_kinds:
            @pl.when(jnp.logical_and(on_parity, is_cur))
            def _(parity=parity, cur_bias=cur_bias):
                _attn_logits(maps, *slots[parity], cur_bias())
                _attn_accumulate(maps, *slots[1 - parity], vt_ref, bias_scalar())

    @pl.when(j == nk)
    def _():
        _attn_accumulate(maps, *slots[(nk - 1) % 2], vt_ref, bias_scalar())
        finalize()


def _diff_attention(qt, k, vt, positions, rel_table, lq1, lk1, lq2, lk2, subln_g, tq, tk):
    S = k.shape[0]
    H = N_HEADS
    nq, nk = S // tq, S // tk
    pos = positions.reshape(S).astype(jnp.int32)
    qmin = pos.reshape(nq, tq).min(axis=1)
    qmax = pos.reshape(nq, tq).max(axis=1)
    kmin = pos.reshape(nk, tk).min(axis=1)
    kmax = pos.reshape(nk, tk).max(axis=1)
    tab_rows = jnp.zeros((H, LANES), F32).at[:, :REL_BUCKETS].set(rel_table.T)

    buckets = pl.pallas_call(
        _bucket_kernel,
        grid_spec=pltpu.PrefetchScalarGridSpec(
            num_scalar_prefetch=4,
            grid=(nk, nq),
            in_specs=[pl.BlockSpec((1, tq), lambda ki, qi, *_: (0, qi)),
                      pl.BlockSpec((tk, 1), lambda ki, qi, *_: (ki, 0))],
            out_specs=pl.BlockSpec((tk, tq), lambda ki, qi, *_: (ki, qi)),
        ),
        out_shape=jax.ShapeDtypeStruct((S, S), jnp.int8),
        compiler_params=_cparams(("parallel", "arbitrary")),
        name="rel_buckets",
    )(qmin, qmax, kmin, kmax, pos.reshape(1, S), pos.reshape(S, 1))

    per_elem = jnp.logical_not(jnp.logical_or(
        kmin[None, :] - qmax[:, None] >= REL_MAX_DIST,
        kmax[None, :] - qmin[:, None] <= -REL_MAX_DIST))
    tile_of_step = jnp.minimum(jnp.arange(nk + 1, dtype=jnp.int32), nk - 1)
    wanted = jnp.where(per_elem[:, tile_of_step], tile_of_step[None, :], -1)
    held = lax.cummax(wanted, axis=1)
    first = jnp.argmax(per_elem, axis=1).astype(jnp.int32)
    bkt_blk = jnp.where(held >= 0, held, first[:, None]).reshape(-1)

    small = jnp.zeros((SUBLANES, V_DIM), F32)
    for row, vec in zip(SMALL_LAMBDA_ROWS, (lq1, lk1, lq2, lk2)):
        small = small.at[row, :HEAD_DIM].set(vec)
    small = small.at[SMALL_GAIN_ROW, :].set(subln_g)

    cur = lambda j: jnp.minimum(j, nk - 1)
    prev = lambda j: jnp.maximum(j - 1, 0)
    in_specs = [
        pl.BlockSpec(memory_space=pltpu.SMEM),
        pl.BlockSpec((H, LANES), lambda h, qi, j, *_: (0, 0)),
        pl.BlockSpec((tk, tq),
                     lambda h, qi, j, a, b, c, d, blk: (blk[qi * (nk + 1) + j], qi)),
        pl.BlockSpec((SUBLANES, V_DIM), lambda h, qi, j, *_: (0, 0)),
        pl.BlockSpec((2 * HEAD_DIM, tq), lambda h, qi, j, *_: (h, qi)),
        pl.BlockSpec((tk, 2 * HEAD_DIM), lambda h, qi, j, *_: (cur(j), h)),
        pl.BlockSpec((V_DIM, tk), lambda h, qi, j, *_: (h, prev(j))),
    ]
    state = [pltpu.VMEM((1, tq), F32), pltpu.VMEM((1, tq), F32), pltpu.VMEM((V_DIM, tq), F32)]
    slot = [pltpu.VMEM((tk, tq), F32), pltpu.VMEM((tk, tq), F32),
            pltpu.VMEM((1, tq), F32), pltpu.VMEM((1, tq), F32)]
    grid_spec = pltpu.PrefetchScalarGridSpec(
        num_scalar_prefetch=5,
        grid=(H, nq, nk + 1),
        in_specs=in_specs,
        out_specs=pl.BlockSpec((tq, V_DIM), lambda h, qi, j, *_: (qi, h)),
        scratch_shapes=state + state + slot + slot,
    )
    return pl.pallas_call(
        functools.partial(_attn_kernel, nk=nk),
        grid_spec=grid_spec,
        out_shape=jax.ShapeDtypeStruct((S, H * V_DIM), BF16),
        compiler_params=_cparams(("parallel", "parallel", "arbitrary")),
        name="diff_attn",
    )(qmin, qmax, kmin, kmax, bkt_blk,
      rel_table, tab_rows, buckets, small,
      qt, k, vt)


HALO = 2 * SUBLANES
CONV_COLS_CHUNK = 2 * LANES
CONV_ROWS_CHUNK = 64


def _conv_kernel(prev_ref, cur_ref, next_ref, w_ref, b_ref, g_ref, beta_ref,
                 o_ref, buf_ref, shift_ref, conv_ref):
    i = pl.program_id(0)
    ts, C = cur_ref.shape
    buf_ref[0:HALO, :] = jnp.where(i > 0, prev_ref[...], 0.0)
    buf_ref[HALO:HALO + ts, :] = cur_ref[...]
    buf_ref[HALO + ts:, :] = jnp.where(i < pl.num_programs(0) - 1, next_ref[...], 0.0)
    n_shift = shift_ref.shape[1]

    def col_body(cc, carry):
        c0 = pl.multiple_of(cc * CONV_COLS_CHUNK, CONV_COLS_CHUNK)
        cols = pl.ds(c0, CONV_COLS_CHUNK)
        for r in range(1, SUBLANES):
            shift_ref[r - 1] = buf_ref[r:r + n_shift, cols]
        for r0 in range(0, ts, CONV_ROWS_CHUNK):
            acc = jnp.broadcast_to(b_ref[:, cols], (CONV_ROWS_CHUNK, CONV_COLS_CHUNK))
            for j in range(CONV_WIDTH):
                start = HALO - CONV_PAD + j
                r, base = start % SUBLANES, start - start % SUBLANES + r0
                if r == 0:
                    taps = buf_ref[base:base + CONV_ROWS_CHUNK, cols]
                else:
                    taps = shift_ref[r - 1, base:base + CONV_ROWS_CHUNK, :]
                acc = acc + w_ref[j:j + 1, cols] * taps
            conv_ref[r0:r0 + CONV_ROWS_CHUNK, cols] = acc
        return carry

    lax.fori_loop(0, C // CONV_COLS_CHUNK, col_body, 0)

    c = conv_ref[...]
    mu = jnp.mean(c, axis=-1, keepdims=True)
    d = c - mu
    var = jnp.mean(d * d, axis=-1, keepdims=True)
    y = d * lax.rsqrt(var + EPS) * g_ref[...] + beta_ref[...]
    o_ref[...] = (y * _sigmoid(y)).astype(o_ref.dtype)


def _conv_module(c, w_dw, b_dw, ln_g, ln_b, ts=256):
    S, C = c.shape
    hb = ts // HALO
    last = S // HALO - 1
    vec = pl.BlockSpec((1, C), lambda i: (0, 0))
    return pl.pallas_call(
        _conv_kernel,
        grid=(S // ts,),
        in_specs=[
            pl.BlockSpec((HALO, C), lambda i: (jnp.maximum(i * hb - 1, 0), 0)),
            pl.BlockSpec((ts, C), lambda i: (i, 0)),
            pl.BlockSpec((HALO, C), lambda i: (jnp.minimum((i + 1) * hb, last), 0)),
            pl.BlockSpec((CONV_WIDTH, C), lambda i: (0, 0)),
            vec, vec, vec,
        ],
        out_specs=pl.BlockSpec((ts, C), lambda i: (i, 0)),
        out_shape=jax.ShapeDtypeStruct((S, C), BF16),
        scratch_shapes=[pltpu.VMEM((ts + 2 * HALO, C), F32),
                        pltpu.VMEM((SUBLANES - 1, ts + 2 * HALO - SUBLANES, CONV_COLS_CHUNK), F32),
                        pltpu.VMEM((ts, C), F32)],
        compiler_params=_cparams(("parallel",)),
        name="conv_module",
    )(c, c, c, w_dw.reshape(CONV_WIDTH, C), b_dw.reshape(1, C),
      ln_g.reshape(1, C), ln_b.reshape(1, C))


def kernel(x, p, positions, rel_table, mix_pre_g, w_in, lambda_q1, lambda_k1, lambda_q2, lambda_k2, subln_g, w_attn_o, w_dw, b_dw, conv_ln_g, conv_ln_b, w_conv_o, w_out, mix_post_g, ffn_pre_g, w_up, w_down, ffn_post_g, w_ple_gate, w_ple_proj, ple_post_g):
    B, S, D = x.shape
    assert B == 1 and w_in.shape[0] == 1
    x2d = x.reshape(S, D)
    qkv_cols = 3 * N_HEADS * 2 * HEAD_DIM
    conv_off = qkv_cols
    gate_off = qkv_cols + 2 * D

    p_b = p[0, 0].astype(BF16)

    h = _prenorm(x2d, mix_pre_g[0])
    head_cols = N_HEADS * 2 * HEAD_DIM
    qt = _fused_mm([(h, w_in[0], 0)], [], lambda a: a, head_cols, BF16, name="q_proj",
                   transpose_out=True, **MM_TILES)
    k = _fused_mm([(h, w_in[0], head_cols)], [], lambda a: a, head_cols, BF16, name="k_proj",
                  **MM_TILES)
    vt = _fused_mm([(h, w_in[0], 2 * head_cols)], [], lambda a: a, head_cols, BF16,
                   name="v_proj", transpose_out=True, **MM_TILES)
    glu = _fused_mm([(h, w_in[0], conv_off), (h, w_in[0], conv_off + D)], [],
                    lambda a, b: a * _sigmoid(b), D, F32, name="conv_glu",
                    **MM_TILES_TWO_WEIGHTS)
    gates = _fused_mm([(h, w_in[0], gate_off)], [], _sigmoid, 2 * D, F32,
                      name="merge_gates", **MM_TILES)

    o = _diff_attention(qt, k, vt, positions, rel_table, lambda_q1[0], lambda_k1[0],
                        lambda_q2[0], lambda_k2[0], subln_g[0], **ATTN_TILES)
    c = _conv_module(glu, w_dw[0], b_dw[0], conv_ln_g[0], conv_ln_b[0])

    merged = _fused_mm([(o, w_attn_o[0], 0), (c, w_conv_o[0], 0)],
                       [(gates, 0), (gates, D)],
                       lambda ya, yc, ga, gc: ga * ya + gc * yc,
                       D, BF16, name="branch_merge", **MM_TILES_TWO_LHS)
    mix = _fused_mm([(merged, w_out[0], 0)], [], lambda a: a, D, F32, name="mix_out",
                    **MM_TILES)
    x1, h2 = _postnorm_res(x2d, mix, mix_post_g[0], ffn_pre_g[0])

    u = _fused_mm([(h2, w_up[0], 0)], [], lambda a: jnp.square(jnp.maximum(a, 0.0)),
                  w_up.shape[2], BF16, name="ffn_up", **MM_TILES)
    dn = _mm_ktiled(u, w_down[0], name="ffn_down", **FFN_DOWN_TILES)
    x2, x2_b = _postnorm_res(x1, dn, ffn_post_g[0])

    ple = _fused_mm([(x2_b, w_ple_gate[0], 0), (p_b, w_ple_proj[0], 0)], [],
                    lambda a, e: _sigmoid(a) * e, D, F32, name="ple_gate", **MM_TILES)
    out = _final_res(x2, ple, ple_post_g[0])
    return out.reshape(B, S, D)
```
